```python
import math
import jax, jax.numpy as jnp
from jax import lax
import numpy as np

D_MODEL = 1024
BATCH = 4
SEQ = 4096
DEPTH = 4

MEM_LEN = 256
NORM_EPS = 1e-6
GDN_HEADS = 8
GDN_DK = 128
GDN_DV = 128
GDN_CONV = 4
GDN_CHUNK = 64
DIFF_HEADS = 8
DIFF_DH = 64
DIFF_Q_BLOCK = 128
MEM_HEADS = 4
MEM_DH = 256
N_BRANCH = 3

GDN_W = GDN_HEADS * GDN_DV
GDN_QKV_W = 2 * GDN_HEADS * GDN_DK + GDN_W
DIFF_W = DIFF_HEADS * 2 * DIFF_DH
MEM_W = MEM_HEADS * MEM_DH
IN_SPLITS = (GDN_QKV_W, GDN_HEADS, GDN_HEADS, GDN_W, DIFF_W, DIFF_W, DIFF_W, DIFF_W, MEM_W, MEM_W, N_BRANCH * D_MODEL)
IN_COLS = sum(IN_SPLITS)

kernel_name = "hybrid_gdn_diffattn_memxattn_gated_merge"


def rms_norm(x, w):
    xf = x.astype(jnp.float32)
    y = xf * lax.rsqrt(jnp.mean(xf * xf, axis=-1, keepdims=True) + NORM_EPS)
    return (y * w.astype(jnp.float32)).astype(x.dtype)


def l2_normalize(t):
    t = t.astype(jnp.float32)
    return t * lax.rsqrt(jnp.sum(t * t, axis=-1, keepdims=True) + NORM_EPS)


def causal_depthwise_conv(x, w):
    c = x.shape[-1]
    k = w.shape[0]
    return lax.conv_general_dilated(x, w[:, None, :].astype(x.dtype), window_strides=(1,), padding=[(k - 1, 0)], dimension_numbers=("NWC", "WIO", "NWC"), feature_group_count=c)


def gated_delta_rule_chunked(q, k, v, g, beta):
    b, s, h, dk = q.shape
    dv = v.shape[-1]
    c = GDN_CHUNK
    n = s // c
    chunk4 = lambda t: t.reshape(b, n, c, h, t.shape[-1]).transpose(0, 1, 3, 2, 4)
    chunk3 = lambda t: t.reshape(b, n, c, h).transpose(0, 1, 3, 2)
    qc, kc, vc = chunk4(q), chunk4(k), chunk4(v)
    gc, bc = chunk3(g), chunk3(beta)
    G = jnp.cumsum(gc, axis=-1)
    idx = jnp.arange(c)
    incl = idx[:, None] >= idx[None, :]
    strict = idx[:, None] > idx[None, :]
    gamma = jnp.exp(jnp.where(incl, G[..., :, None] - G[..., None, :], -jnp.inf))
    kk = jnp.einsum("bnhcd,bnhed->bnhce", kc, kc)
    a_low = jnp.where(strict, bc[..., :, None] * kk * gamma, 0.0)
    lmat = a_low + jnp.eye(c, dtype=jnp.float32)
    rhs = jnp.concatenate([vc * bc[..., None], kc * (bc * jnp.exp(G))[..., None]], axis=-1)
    sol = lax.linalg.triangular_solve(lmat, rhs, left_side=True, lower=True, unit_diagonal=True)
    u, w = sol[..., :dv], sol[..., dv:]
    attn_intra = jnp.einsum("bnhcd,bnhed->bnhce", qc, kc) * gamma
    q_dec = qc * jnp.exp(G)[..., None]
    k_dec = kc * jnp.exp(G[..., -1:] - G)[..., None]
    g_last = jnp.exp(G[..., -1])
    xs = tuple(jnp.moveaxis(t, 1, 0) for t in (u, w, q_dec, k_dec, attn_intra, g_last))

    def step(state, inp):
        u_i, w_i, qd_i, kd_i, a_i, gl_i = inp
        v_new = u_i - jnp.einsum("bhcd,bhde->bhce", w_i, state)
        o_i = jnp.einsum("bhcd,bhde->bhce", qd_i, state) + jnp.einsum("bhce,bhef->bhcf", a_i, v_new)
        state = state * gl_i[..., None, None] + jnp.einsum("bhcd,bhce->bhde", kd_i, v_new)
        return state, o_i

    s0 = jnp.zeros((b, h, dk, dv), jnp.float32)
    _, o = lax.scan(step, s0, xs)
    return o.transpose(1, 0, 3, 2, 4).reshape(b, s, h, dv)


def alibi_slopes(n_heads):
    return jnp.asarray(2.0 ** (-8.0 * np.arange(1, n_heads + 1) / n_heads), jnp.float32)


def diff_attention_causal(q, k, v, lam, slopes):
    s_len = q.shape[1]
    scale = q.shape[-1] ** -0.5
    outs = []
    for start in range(0, s_len, DIFF_Q_BLOCK):
        end = start + DIFF_Q_BLOCK
        sc = jnp.einsum("bqhmd,bkhmd->bhmqk", q[:, start:end], k[:, :end], preferred_element_type=jnp.float32) * scale
        dist = jnp.arange(start, end, dtype=jnp.float32)[:, None] - jnp.arange(end, dtype=jnp.float32)[None, :]
        bias = jnp.where(dist >= 0, -slopes[:, None, None] * dist, -jnp.inf)
        p = jax.nn.softmax(sc + bias[None, :, None], axis=-1)
        pd = (p[:, :, 0] - lam * p[:, :, 1]).astype(v.dtype)
        outs.append(jnp.einsum("bhqk,bkhe->bqhe", pd, v[:, :end]))
    return jnp.concatenate(outs, axis=1)


def hybrid_layer(x, mem, layer_idx, pre_w, post_w, w_in, conv_w, a_log, dt_bias, gdn_norm_w, lam_vecs, diff_norm_w, mem_norm_w, w_mem_kv, w_br_gdn, w_br_diff, w_br_mem, w_out):
    b, s, d = x.shape
    f32 = jnp.float32
    h = rms_norm(x, pre_w)
    bounds = []
    off = 0
    for width in IN_SPLITS:
        bounds.append((off, off + width))
        off += width
    (gdn_qkv, gdn_a, gdn_b, gdn_z, dq, dk, dv, dz, mq, mz, gate_logits) = [h @ w_in[:, lo:hi] for lo, hi in bounds]

    qkv = jax.nn.silu(causal_depthwise_conv(gdn_qkv, conv_w))
    qa, ka, va = jnp.split(qkv, [GDN_HEADS * GDN_DK, 2 * GDN_HEADS * GDN_DK], axis=-1)
    qa = l2_normalize(qa.reshape(b, s, GDN_HEADS, GDN_DK)) * (GDN_DK ** -0.5)
    ka = l2_normalize(ka.reshape(b, s, GDN_HEADS, GDN_DK))
    va = va.reshape(b, s, GDN_HEADS, GDN_DV).astype(f32)
    beta = jax.nn.sigmoid(gdn_b.astype(f32))
    g = -jnp.exp(a_log.astype(f32)) * jax.nn.softplus(gdn_a.astype(f32) + dt_bias.astype(f32))
    oa = gated_delta_rule_chunked(qa, ka, va, g, beta)
    oa = rms_norm(oa, gdn_norm_w) * jax.nn.silu(gdn_z.reshape(b, s, GDN_HEADS, GDN_DV))
    y_gdn = oa.reshape(b, s, GDN_W).astype(x.dtype) @ w_br_gdn

    lam_init = 0.8 - 0.6 * math.exp(-0.3 * layer_idx)
    lv = lam_vecs.astype(f32)
    lam = jnp.exp(jnp.sum(lv[0] * lv[1])) - jnp.exp(jnp.sum(lv[2] * lv[3])) + lam_init
    qb = dq.reshape(b, s, DIFF_HEADS, 2, DIFF_DH)
    kb = dk.reshape(b, s, DIFF_HEADS, 2, DIFF_DH)
    vb = dv.reshape(b, s, DIFF_HEADS, 2 * DIFF_DH)
    ob = diff_attention_causal(qb, kb, vb, lam, alibi_slopes(DIFF_HEADS))
    ob = rms_norm(ob, diff_norm_w) * (1.0 - lam_init)
    ob = ob * jax.nn.silu(dz.reshape(b, s, DIFF_HEADS, 2 * DIFF_DH))
    y_diff = ob.reshape(b, s, DIFF_W) @ w_br_diff

    m = rms_norm(mem, mem_norm_w)
    mk, mv = jnp.split(m @ w_mem_kv, 2, axis=-1)
    mk = mk.reshape(b, -1, MEM_HEADS, MEM_DH)
    mv = mv.reshape(b, -1, MEM_HEADS, MEM_DH)
    qc = mq.reshape(b, s, MEM_HEADS, MEM_DH)
    sc = jnp.einsum("bshd,bmhd->bhsm", qc, mk, preferred_element_type=f32) * (MEM_DH ** -0.5)
    p = jax.nn.softmax(sc, axis=-1).astype(mv.dtype)
    oc = jnp.einsum("bhsm,bmhd->bshd", p, mv).reshape(b, s, MEM_W) * jax.nn.silu(mz)
    y_mem = oc @ w_br_mem

    gates = jax.nn.sigmoid(gate_logits).reshape(b, s, N_BRANCH, d)
    y = gates[:, :, 0] * y_gdn + gates[:, :, 1] * y_diff + gates[:, :, 2] * y_mem
    out = y @ w_out
    return x + rms_norm(out, post_w)


def setup_inputs(seed: int = 0) -> dict:
    key = jax.random.key(seed)
    ks = jax.random.split(key, 20)
    f32 = jnp.float32
    nrm = lambda k, shape, scale: jax.random.normal(k, shape, f32) * scale
    x = nrm(ks[0], (BATCH, SEQ, D_MODEL), 1.0)
    mem = nrm(ks[1], (BATCH, MEM_LEN, D_MODEL), 1.0)
    pre_norm_w = 1.0 + nrm(ks[2], (DEPTH, D_MODEL), 0.02)
    post_norm_w = 1.0 + nrm(ks[3], (DEPTH, D_MODEL), 0.02)
    w_in = nrm(ks[4], (DEPTH, D_MODEL, IN_COLS), D_MODEL ** -0.5)
    gdn_conv_w = nrm(ks[5], (DEPTH, GDN_CONV, GDN_QKV_W), GDN_CONV ** -0.5)
    gdn_a_log = jnp.log(jax.random.uniform(ks[6], (DEPTH, GDN_HEADS), f32, 1.0, 16.0))
    dt = jnp.exp(jax.random.uniform(ks[7], (DEPTH, GDN_HEADS), f32, math.log(1e-3), math.log(1e-1)))
    gdn_dt_bias = dt + jnp.log(-jnp.expm1(-dt))
    gdn_norm_w = 1.0 + nrm(ks[8], (DEPTH, GDN_DV), 0.02)
    diff_lambda = nrm(ks[9], (DEPTH, 4, DIFF_DH), 0.1)
    diff_norm_w = 1.0 + nrm(ks[10], (DEPTH, 2 * DIFF_DH), 0.02)
    mem_norm_w = 1.0 + nrm(ks[11], (DEPTH, D_MODEL), 0.02)
    w_mem_kv = nrm(ks[12], (DEPTH, D_MODEL, 2 * MEM_W), D_MODEL ** -0.5)
    w_br_gdn = nrm(ks[13], (DEPTH, GDN_W, D_MODEL), GDN_W ** -0.5)
    w_br_diff = nrm(ks[14], (DEPTH, DIFF_W, D_MODEL), DIFF_W ** -0.5)
    w_br_mem = nrm(ks[15], (DEPTH, MEM_W, D_MODEL), MEM_W ** -0.5)
    w_out = nrm(ks[16], (DEPTH, D_MODEL, D_MODEL), D_MODEL ** -0.5)
    return {"x": x, "mem": mem, "pre_norm_w": pre_norm_w, "post_norm_w": post_norm_w, "w_in": w_in, "gdn_conv_w": gdn_conv_w, "gdn_a_log": gdn_a_log, "gdn_dt_bias": gdn_dt_bias, "gdn_norm_w": gdn_norm_w, "diff_lambda": diff_lambda, "diff_norm_w": diff_norm_w, "mem_norm_w": mem_norm_w, "w_mem_kv": w_mem_kv, "w_br_gdn": w_br_gdn, "w_br_diff": w_br_diff, "w_br_mem": w_br_mem, "w_out": w_out}


def reference(x, mem, pre_norm_w, post_norm_w, w_in, gdn_conv_w, gdn_a_log, gdn_dt_bias, gdn_norm_w, diff_lambda, diff_norm_w, mem_norm_w, w_mem_kv, w_br_gdn, w_br_diff, w_br_mem, w_out):
    for l in range(DEPTH):
        x = hybrid_layer(x, mem, l, pre_norm_w[l], post_norm_w[l], w_in[l], gdn_conv_w[l], gdn_a_log[l], gdn_dt_bias[l], gdn_norm_w[l], diff_lambda[l], diff_norm_w[l], mem_norm_w[l], w_mem_kv[l], w_br_gdn[l], w_br_diff[l], w_br_mem[l], w_out[l])
    return x
```

```python
import functools
import math

import jax
import jax.numpy as jnp
from jax import lax
from jax.experimental import pallas as pl
from jax.experimental.pallas import tpu as pltpu

F32 = jnp.float32
BF16 = jnp.bfloat16

D_MODEL = 1024
NORM_EPS = 1e-6
GDN_HEADS = 8
GDN_D = 128
GDN_CONV = 4
GDN_CHUNK = 64
DIFF_HEADS = 8
DIFF_DH = 64
DIFF_DV = 2 * DIFF_DH
MEM_HEADS = 4
MEM_DH = 256
N_BRANCH = 3

LANES = 128
QKV_W = 3 * GDN_HEADS * GDN_D
COL_GDN_Q, COL_GDN_K, COL_GDN_V, COL_GDN_Z = 0, 1, 2, 3
COL_DIFF_Q, COL_DIFF_K, COL_DIFF_V, COL_DIFF_Z = 32, 40, 48, 56
COL_MEM_Q, COL_MEM_Z = 8, 9
COL_GATES = 10
PROJ_W = 13 * D_MODEL
AB_W = 2 * LANES

VMEM_LIMIT = 56 * 1024 * 1024


def _sigmoid(x):
    return 1.0 / (1.0 + jnp.exp(-x))


def _softplus(x):
    return jnp.maximum(x, 0.0) + jnp.log(1.0 + jnp.exp(-jnp.abs(x)))


def _dot(a, b):
    return jnp.dot(a, b, preferred_element_type=F32)


def _dot_nt(a, b):
    return lax.dot_general(a, b, (((1,), (1,)), ((), ())), preferred_element_type=F32)


def _dot_tn(a, b):
    return lax.dot_general(a, b, (((0,), (0,)), ((), ())), preferred_element_type=F32)


def _inproj_kernel(x_ref, nw_ref, w_ref, wab_ref, o_ref, oab_ref, h_scr):
    @pl.when(pl.program_id(1) == 0)
    def _():
        x = x_ref[...]
        ms = jnp.mean(x * x, axis=-1, keepdims=True)
        hb = ((x * lax.rsqrt(ms + NORM_EPS)) * nw_ref[...]).astype(BF16)
        h_scr[...] = hb
        oab_ref[...] = _dot(hb, wab_ref[...])

    o_ref[...] = _dot(h_scr[...], w_ref[...]).astype(o_ref.dtype)


def _inproj(x, nw, w, wab, *, tm=1024, tn=1024):
    m = x.shape[0]
    return pl.pallas_call(
        _inproj_kernel,
        grid=(m // tm, PROJ_W // tn),
        in_specs=[
            pl.BlockSpec((tm, D_MODEL), lambda i, j: (i, 0)),
            pl.BlockSpec((1, D_MODEL), lambda i, j: (0, 0)),
            pl.BlockSpec((D_MODEL, tn), lambda i, j: (0, j)),
            pl.BlockSpec((D_MODEL, AB_W), lambda i, j: (0, 0)),
        ],
        out_specs=[
            pl.BlockSpec((tm, tn), lambda i, j: (i, j)),
            pl.BlockSpec((tm, AB_W), lambda i, j: (i, 0)),
        ],
        out_shape=[
            jax.ShapeDtypeStruct((m, PROJ_W), BF16),
            jax.ShapeDtypeStruct((m, AB_W), F32),
        ],
        scratch_shapes=[pltpu.VMEM((tm, D_MODEL), BF16)],
        compiler_params=pltpu.CompilerParams(
            dimension_semantics=("arbitrary", "arbitrary"), vmem_limit_bytes=VMEM_LIMIT),
    )(x, nw, w, wab)


def _gdn_kernel(q_ref, k_ref, v_ref, z_ref, a_ref, b_ref, cw_ref, alog_ref, dtb_ref, nw_ref, o_ref,
                state_scr, carry_scr, qs, ks, vs, gcol, bcol, grow, brow, *, t_blk):
    c_len = GDN_CHUNK
    n_chunk = t_blk // c_len
    hw = GDN_HEADS * GDN_D

    @pl.when(pl.program_id(1) == 0)
    def _():
        state_scr[...] = jnp.zeros_like(state_scr)
        carry_scr[...] = jnp.zeros_like(carry_scr)

    def conv_silu(ref, idx):
        x = ref[...].astype(F32)
        xp = jnp.concatenate([carry_scr[idx], x], axis=0)
        w = cw_ref[idx]
        y = w[0:1] * xp[5:5 + t_blk]
        for j in range(1, GDN_CONV):
            y = y + w[j:j + 1] * xp[5 + j:5 + j + t_blk]
        carry_scr[idx] = x[t_blk - 8:t_blk]
        return y * _sigmoid(y)

    yq = conv_silu(q_ref, 0)
    yk = conv_silu(k_ref, 1)
    vs[...] = conv_silu(v_ref, 2)
    for h in range(GDN_HEADS):
        sl = slice(h * GDN_D, (h + 1) * GDN_D)
        qh = yq[:, sl]
        kh = yk[:, sl]
        qs[:, sl] = qh * (lax.rsqrt(jnp.sum(qh * qh, axis=-1, keepdims=True) + NORM_EPS) * (GDN_D ** -0.5))
        ks[:, sl] = kh * lax.rsqrt(jnp.sum(kh * kh, axis=-1, keepdims=True) + NORM_EPS)

    g = -jnp.exp(alog_ref[...]) * _softplus(a_ref[...] + dtb_ref[...])
    beta = _sigmoid(b_ref[...])
    rin = lax.broadcasted_iota(jnp.int32, (t_blk, LANES), 0) & (c_len - 1)
    s = 1
    while s < c_len:
        g = g + jnp.where(rin >= s, pltpu.roll(g, s, axis=0), 0.0)
        s *= 2
    gcol[...] = g
    bcol[...] = beta
    g_t = g.T
    b_t = beta.T
    for c in range(n_chunk):
        grow[c] = g_t[0:8, c * c_len:(c + 1) * c_len]
        brow[c] = b_t[0:8, c * c_len:(c + 1) * c_len]

    ii = lax.broadcasted_iota(jnp.int32, (c_len, c_len), 0)
    jj = lax.broadcasted_iota(jnp.int32, (c_len, c_len), 1)
    incl = ii >= jj
    strict = ii > jj
    eye = jnp.where(ii == jj, 1.0, 0.0).astype(F32)
    lane_lo = lax.broadcasted_iota(jnp.int32, (c_len, 2 * c_len), 1) < c_len
    nw = nw_ref[...]

    def chunk_body(c, _):
        r0 = pl.multiple_of(c * c_len, c_len)
        rows = pl.ds(r0, c_len)
        g_c = gcol[rows, :]
        b_c = bcol[rows, :]
        g_r = grow[c]
        b_r = brow[c]
        for h in range(GDN_HEADS):
            sl = slice(h * GDN_D, (h + 1) * GDN_D)
            q = qs[rows, sl]
            k = ks[rows, sl]
            v = vs[rows, sl]
            gi = g_c[:, h:h + 1]
            gj = g_r[h:h + 1, :]
            bi = b_c[:, h:h + 1]
            bj = b_r[h:h + 1, :]
            g_last = g_c[c_len - 1:c_len, h:h + 1]
            gam = jnp.exp(jnp.where(incl, gi - gj, -jnp.inf))
            kb = k.astype(BF16)
            qk = _dot_nt(jnp.concatenate([q.astype(BF16), kb], axis=0), kb)
            attn = qk[:c_len] * gam
            a_mat = jnp.where(strict, bi * qk[c_len:] * gam, 0.0)
            w = jnp.concatenate([eye, -a_mat], axis=1)
            for _ in range(6):
                bm = w[:, c_len:].astype(BF16)
                r = _dot(bm, w.astype(BF16))
                w = jnp.where(lane_lo, w + r, r)
            t_mat = (w[:, :c_len] * bj).astype(BF16)
            eg = jnp.exp(gi)
            uw = _dot(t_mat, jnp.concatenate([v, k * eg], axis=1).astype(BF16))
            st = state_scr[h]
            wq = _dot(jnp.concatenate([uw[:, GDN_D:], q * eg], axis=0).astype(BF16), st.astype(BF16))
            v_new = (uw[:, :GDN_D] - wq[:c_len]).astype(BF16)
            o = wq[c_len:] + _dot(attn.astype(BF16), v_new)
            kd = (k * jnp.exp(g_last - gi)).astype(BF16)
            state_scr[h] = st * jnp.exp(g_last) + _dot_tn(kd, v_new)
            on = o * lax.rsqrt(jnp.mean(o * o, axis=-1, keepdims=True) + NORM_EPS) * nw
            z = z_ref[rows, sl].astype(F32)
            o_ref[rows, sl] = (on * (z * _sigmoid(z))).astype(o_ref.dtype)
        return 0

    lax.fori_loop(0, n_chunk, chunk_body, 0)


def _gdn(proj, ab, cw, alog, dtb, nw, *, batch, seq, t_blk=256):
    m = batch * seq
    hw = GDN_HEADS * GDN_D
    nt = seq // t_blk
    n_chunk = t_blk // GDN_CHUNK
    tok = lambda col: pl.BlockSpec((t_blk, hw), lambda b, t, col=col: (b * nt + t, col))
    const2 = lambda shape: pl.BlockSpec(shape, lambda b, t: (0,) * len(shape))
    return pl.pallas_call(
        functools.partial(_gdn_kernel, t_blk=t_blk),
        grid=(batch, nt),
        in_specs=[
            tok(COL_GDN_Q), tok(COL_GDN_K), tok(COL_GDN_V), tok(COL_GDN_Z),
            pl.BlockSpec((t_blk, LANES), lambda b, t: (b * nt + t, 0)),
            pl.BlockSpec((t_blk, LANES), lambda b, t: (b * nt + t, 1)),
            const2((3, GDN_CONV, hw)), const2((1, LANES)), const2((1, LANES)), const2((1, GDN_D)),
        ],
        out_specs=pl.BlockSpec((t_blk, hw), lambda b, t: (b * nt + t, 0)),
        out_shape=jax.ShapeDtypeStruct((m, hw), BF16),
        scratch_shapes=[
            pltpu.VMEM((GDN_HEADS, GDN_D, GDN_D), F32),
            pltpu.VMEM((3, 8, hw), F32),
            pltpu.VMEM((t_blk, hw), F32), pltpu.VMEM((t_blk, hw), F32), pltpu.VMEM((t_blk, hw), F32),
            pltpu.VMEM((t_blk, LANES), F32), pltpu.VMEM((t_blk, LANES), F32),
            pltpu.VMEM((n_chunk, 8, GDN_CHUNK), F32), pltpu.VMEM((n_chunk, 8, GDN_CHUNK), F32),
        ],
        compiler_params=pltpu.CompilerParams(
            dimension_semantics=("arbitrary", "arbitrary"), vmem_limit_bytes=VMEM_LIMIT),
    )(proj, proj, proj, proj, ab, ab, cw, alog, dtb, nw)


def _diff_kernel(lam_ref, nw_ref, slope_ref, q_ref, k_ref, v_ref, z_ref, o_ref, vt_scr, acc_scr,
                 *, tq, lam_init):
    tk = tq
    qi = pl.program_id(2)

    @pl.when(qi == 0)
    def _():
        vt_scr[...] = v_ref[...].astype(F32).T.astype(BF16)

    q_t = (q_ref[...].astype(F32) * (DIFF_DH ** -0.5)).T
    row = lax.broadcasted_iota(jnp.int32, (2 * DIFF_DH, tq), 0)
    qq = jnp.concatenate([jnp.where(row < DIFF_DH, q_t, 0.0), jnp.where(row >= DIFF_DH, q_t, 0.0)],
                         axis=1).astype(BF16)
    slope = slope_ref[0]
    key_pos = lax.broadcasted_iota(jnp.int32, (tk, 2 * tq), 0)
    bias = slope * key_pos.astype(F32)
    acc_scr[...] = jnp.zeros_like(acc_scr)

    def step(kj, m, l, masked):
        k0 = pl.multiple_of(kj * tk, tk)
        k = k_ref[pl.ds(k0, tk), :]
        s = _dot(k, qq) + bias
        if masked:
            q_pos = lax.broadcasted_iota(jnp.int32, (tk, 2 * tq), 1) & (tq - 1)
            s = jnp.where(key_pos <= q_pos, s, -jnp.inf)
        off = slope[:, 0:1] * (kj * tk).astype(F32)
        m_new = jnp.maximum(m, jnp.max(s, axis=0, keepdims=True) + off)
        p = jnp.exp(s - (m_new - off))
        alpha = jnp.exp(m - m_new)
        l_new = alpha * l + jnp.sum(p, axis=0, keepdims=True)
        acc_scr[...] = acc_scr[...] * alpha + _dot(vt_scr[:, pl.ds(k0, tk)], p.astype(BF16))
        return m_new, l_new

    m0 = jnp.full((1, 2 * tq), -jnp.inf, F32)
    l0 = jnp.zeros((1, 2 * tq), F32)
    m, l = lax.fori_loop(0, qi, lambda kj, c: step(kj, c[0], c[1], False), (m0, l0))
    m, l = step(qi, m, l, True)

    lv = lam_ref[...]
    lam = (jnp.exp(jnp.sum(lv[0:1] * lv[1:2], axis=-1, keepdims=True))
           - jnp.exp(jnp.sum(lv[2:3] * lv[3:4], axis=-1, keepdims=True)) + lam_init)
    acc = acc_scr[...] / l
    o = (acc[:, :tq] - lam * acc[:, tq:]).T
    on = o * lax.rsqrt(jnp.mean(o * o, axis=-1, keepdims=True) + NORM_EPS) * nw_ref[...] * (1.0 - lam_init)
    z = z_ref[...].astype(F32)
    o_ref[...] = (on * (z * _sigmoid(z))).astype(o_ref.dtype)


def _diff_attn(proj, lam_vecs, nw, *, batch, seq, lam_init, tq=256):
    m = batch * seq
    nq = seq // tq
    slopes = 2.0 ** (-8.0 * jnp.arange(1, DIFF_HEADS + 1, dtype=F32) / DIFF_HEADS)
    slopes = jnp.broadcast_to(slopes[:, None, None], (DIFF_HEADS, 1, 2 * tq))
    return pl.pallas_call(
        functools.partial(_diff_kernel, tq=tq, lam_init=lam_init),
        grid=(batch, DIFF_HEADS, nq),
        in_specs=[
            pl.BlockSpec((4, DIFF_DH), lambda b, h, i: (0, 0)),
            pl.BlockSpec((1, DIFF_DV), lambda b, h, i: (0, 0)),
            pl.BlockSpec((1, 1, 2 * tq), lambda b, h, i: (h, 0, 0)),
            pl.BlockSpec((tq, DIFF_DV), lambda b, h, i: (b * nq + i, COL_DIFF_Q + h)),
            pl.BlockSpec((seq, DIFF_DV), lambda b, h, i: (b, COL_DIFF_K + h)),
            pl.BlockSpec((seq, DIFF_DV), lambda b, h, i: (b, COL_DIFF_V + h)),
            pl.BlockSpec((tq, DIFF_DV), lambda b, h, i: (b * nq + i, COL_DIFF_Z + h)),
        ],
        out_specs=pl.BlockSpec((tq, DIFF_DV), lambda b, h, i: (b * nq + i, h)),
        out_shape=jax.ShapeDtypeStruct((m, DIFF_HEADS * DIFF_DV), BF16),
        scratch_shapes=[pltpu.VMEM((DIFF_DV, seq), BF16), pltpu.VMEM((DIFF_DV, 2 * tq), F32)],
        compiler_params=pltpu.CompilerParams(
            dimension_semantics=("arbitrary", "arbitrary", "arbitrary"), vmem_limit_bytes=VMEM_LIMIT),
    )(lam_vecs, nw, slopes, proj, proj, proj, proj)


def _memkv_kernel(mem_ref, nw_ref, w_ref, o_ref):
    x = mem_ref[...]
    ms = jnp.mean(x * x, axis=-1, keepdims=True)
    hb = ((x * lax.rsqrt(ms + NORM_EPS)) * nw_ref[...]).astype(BF16)
    o_ref[...] = _dot(hb, w_ref[...]).astype(o_ref.dtype)


def _memkv(mem2d, nw, w, *, tn=1024):
    rows = mem2d.shape[0]
    width = w.shape[1]
    return pl.pallas_call(
        _memkv_kernel,
        grid=(width // tn,),
        in_specs=[
            pl.BlockSpec((rows, D_MODEL), lambda j: (0, 0)),
            pl.BlockSpec((1, D_MODEL), lambda j: (0, 0)),
            pl.BlockSpec((D_MODEL, tn), lambda j: (0, j)),
        ],
        out_specs=pl.BlockSpec((rows, tn), lambda j: (0, j)),
        out_shape=jax.ShapeDtypeStruct((rows, width), BF16),
        compiler_params=pltpu.CompilerParams(
            dimension_semantics=("arbitrary",), vmem_limit_bytes=VMEM_LIMIT),
    )(mem2d, nw, w)


def _merge_kernel(x_ref, oa_ref, ob_ref, mq_ref, mz_ref, gg_ref, gd_ref, gm_ref, mk_ref, mv_ref,
                  wg_ref, wd_ref, wm_ref, wo_ref, pw_ref, o_ref):
    heads = []
    for h in range(MEM_HEADS):
        sl = slice(h * MEM_DH, (h + 1) * MEM_DH)
        s = _dot_nt(mq_ref[:, sl], mk_ref[:, sl]) * (MEM_DH ** -0.5)
        p = jnp.exp(s - jnp.max(s, axis=-1, keepdims=True))
        p = p / jnp.sum(p, axis=-1, keepdims=True)
        heads.append(_dot(p.astype(BF16), mv_ref[:, sl]))
    mz = mz_ref[...].astype(F32)
    oc = (jnp.concatenate(heads, axis=1) * (mz * _sigmoid(mz))).astype(BF16)

    y = _sigmoid(gg_ref[...].astype(F32)) * _dot(oa_ref[...], wg_ref[...])
    y = y + _sigmoid(gd_ref[...].astype(F32)) * _dot(ob_ref[...], wd_ref[...])
    y = y + _sigmoid(gm_ref[...].astype(F32)) * _dot(oc, wm_ref[...])
    out = _dot(y.astype(BF16), wo_ref[...])
    ms = jnp.mean(out * out, axis=-1, keepdims=True)
    o_ref[...] = x_ref[...] + (out * lax.rsqrt(ms + NORM_EPS)) * pw_ref[...]


def _merge(x, proj, oa, ob, memkv, wg, wd, wm, wo, pw, *, seq, mem_len, tm=512):
    m = x.shape[0]
    per_b = seq // tm
    mem_w = MEM_HEADS * MEM_DH
    tok = lambda width, col: pl.BlockSpec((tm, width), lambda i, col=col: (i, col))
    wspec = pl.BlockSpec((D_MODEL, D_MODEL), lambda i: (0, 0))
    return pl.pallas_call(
        _merge_kernel,
        grid=(m // tm,),
        in_specs=[
            tok(D_MODEL, 0), tok(D_MODEL, 0), tok(D_MODEL, 0),
            tok(mem_w, COL_MEM_Q), tok(mem_w, COL_MEM_Z),
            tok(D_MODEL, COL_GATES), tok(D_MODEL, COL_GATES + 1), tok(D_MODEL, COL_GATES + 2),
            pl.BlockSpec((mem_len, mem_w), lambda i: (i // per_b, 0)),
            pl.BlockSpec((mem_len, mem_w), lambda i: (i // per_b, 1)),
            wspec, wspec, wspec, wspec,
            pl.BlockSpec((1, D_MODEL), lambda i: (0, 0)),
        ],
        out_specs=pl.BlockSpec((tm, D_MODEL), lambda i: (i, 0)),
        out_shape=jax.ShapeDtypeStruct((m, D_MODEL), F32),
        compiler_params=pltpu.CompilerParams(
            dimension_semantics=("arbitrary",), vmem_limit_bytes=VMEM_LIMIT),
    )(x, oa, ob, proj, proj, proj, proj, proj, memkv, memkv, wg, wd, wm, wo, pw)


def kernel(x, mem, pre_norm_w, post_norm_w, w_in, gdn_conv_w, gdn_a_log, gdn_dt_bias, gdn_norm_w,
           diff_lambda, diff_norm_w, mem_norm_w, w_mem_kv, w_br_gdn, w_br_diff, w_br_mem, w_out):
    batch, seq, d = x.shape
    mem_len = mem.shape[1]
    depth = w_in.shape[0]
    assert d == D_MODEL and seq % 512 == 0

    w_main = jnp.concatenate([w_in[:, :, :QKV_W], w_in[:, :, QKV_W + 2 * GDN_HEADS:]], axis=-1).astype(BF16)
    w_a = w_in[:, :, QKV_W:QKV_W + GDN_HEADS]
    w_b = w_in[:, :, QKV_W + GDN_HEADS:QKV_W + 2 * GDN_HEADS]
    pad = jnp.zeros((depth, D_MODEL, LANES - GDN_HEADS), w_in.dtype)
    w_ab = jnp.concatenate([w_a, pad, w_b, pad], axis=-1).astype(BF16)
    hw = GDN_HEADS * GDN_D
    conv_w = gdn_conv_w.reshape(depth, GDN_CONV, 3, hw).transpose(0, 2, 1, 3)
    lane_pad = lambda t: jnp.pad(t, ((0, 0), (0, LANES - GDN_HEADS)))[:, None, :]
    a_log = lane_pad(gdn_a_log)
    dt_bias = lane_pad(gdn_dt_bias)
    w_mem_kv_b = w_mem_kv.astype(BF16)
    w_g, w_d, w_m, w_o = (t.astype(BF16) for t in (w_br_gdn, w_br_diff, w_br_mem, w_out))

    xf = x.reshape(batch * seq, d)
    mem2d = mem.reshape(batch * mem_len, d)
    for l in range(depth):
        lam_init = 0.8 - 0.6 * math.exp(-0.3 * l)
        proj, ab = _inproj(xf, pre_norm_w[l][None], w_main[l], w_ab[l])
        oa = _gdn(proj, ab, conv_w[l], a_log[l], dt_bias[l], gdn_norm_w[l][None], batch=batch, seq=seq)
        ob = _diff_attn(proj, diff_lambda[l], diff_norm_w[l][None], batch=batch, seq=seq, lam_init=lam_init)
        memkv = _memkv(mem2d, mem_norm_w[l][None], w_mem_kv_b[l])
        xf = _merge(xf, proj, oa, ob, memkv, w_g[l], w_d[l], w_m[l], w_o[l], post_norm_w[l][None],
                    seq=seq, mem_len=mem_len)
    return xf.reshape(batch, seq, d)
```

```python
import functools
import math

import jax
import jax.numpy as jnp
from jax import lax
from jax.experimental import pallas as pl
from jax.experimental.pallas import tpu as pltpu

F32 = jnp.float32
BF16 = jnp.bfloat16

D_MODEL = 1024
NORM_EPS = 1e-6
GDN_HEADS = 8
GDN_D = 128
GDN_CONV = 4
GDN_CHUNK = 64
A_CHUNKS_PER_STEP = 2
DIFF_HEADS = 8
DIFF_DH = 64
DIFF_DV = 2 * DIFF_DH
MEM_HEADS = 4
MEM_DH = 256
N_BRANCH = 3

LANES = 128
LOG2E = 1.4426950408889634
QKV_W = 3 * GDN_HEADS * GDN_D
COL_GDN_Q, COL_GDN_K, COL_GDN_V, COL_GDN_Z = 0, 1, 2, 3
COL_DIFF_Q, COL_DIFF_K, COL_DIFF_V, COL_DIFF_Z = 32, 40, 48, 56
COL_MEM_Q, COL_MEM_Z = 8, 9
COL_GATES = 10
PROJ_W = 13 * D_MODEL
AB_W = 2 * LANES

VMEM_LIMIT = 56 * 1024 * 1024


def _sigmoid(x):
    return 1.0 / (1.0 + jnp.exp(-x))


def _softplus(x):
    return jnp.maximum(x, 0.0) + jnp.log(1.0 + jnp.exp(-jnp.abs(x)))


def _dot(a, b):
    return jnp.dot(a, b, preferred_element_type=F32)


def _dot_nt(a, b):
    return lax.dot_general(a, b, (((1,), (1,)), ((), ())), preferred_element_type=F32)


def _dot_tn(a, b):
    return lax.dot_general(a, b, (((0,), (0,)), ((), ())), preferred_element_type=F32)


def _inproj_kernel(x_ref, nw_ref, w_ref, wab_ref, o_ref, oab_ref, h_scr):
    @pl.when(pl.program_id(1) == 0)
    def _():
        x = x_ref[...]
        ms = jnp.mean(x * x, axis=-1, keepdims=True)
        hb = ((x * lax.rsqrt(ms + NORM_EPS)) * nw_ref[...]).astype(BF16)
        h_scr[...] = hb
        oab_ref[...] = _dot(hb, wab_ref[...])

    o_ref[...] = _dot(h_scr[...], w_ref[...]).astype(o_ref.dtype)


def _inproj(x, nw, w, wab, *, tm=1024, tn=1024):
    m = x.shape[0]
    return pl.pallas_call(
        _inproj_kernel,
        grid=(m // tm, PROJ_W // tn),
        in_specs=[
            pl.BlockSpec((tm, D_MODEL), lambda i, j: (i, 0)),
            pl.BlockSpec((1, D_MODEL), lambda i, j: (0, 0)),
            pl.BlockSpec((D_MODEL, tn), lambda i, j: (0, j)),
            pl.BlockSpec((D_MODEL, AB_W), lambda i, j: (0, 0)),
        ],
        out_specs=[
            pl.BlockSpec((tm, tn), lambda i, j: (i, j)),
            pl.BlockSpec((tm, AB_W), lambda i, j: (i, 0)),
        ],
        out_shape=[
            jax.ShapeDtypeStruct((m, PROJ_W), BF16),
            jax.ShapeDtypeStruct((m, AB_W), F32),
        ],
        scratch_shapes=[pltpu.VMEM((tm, D_MODEL), BF16)],
        name="inproj",
        compiler_params=pltpu.CompilerParams(
            dimension_semantics=("arbitrary", "arbitrary"), vmem_limit_bytes=VMEM_LIMIT),
    )(x, nw, w, wab)


def _gdn_kernel(q_ref, k_ref, v_ref, z_ref, a_ref, b_ref, cw_ref, alog_ref, dtb_ref, nw_ref, o_ref,
                state_scr, carry_scr, qs, ks, vs, gcol, bcol, grow, brow, us, wqe, attn_s, kd_s, *, t_blk):
    c_len = GDN_CHUNK
    n_chunk = t_blk // c_len
    hw = GDN_HEADS * GDN_D

    @pl.when(pl.program_id(1) == 0)
    def _():
        state_scr[...] = jnp.zeros_like(state_scr)
        carry_scr[...] = jnp.zeros_like(carry_scr)

    def conv_silu(ref, idx):
        x = ref[...].astype(F32)
        xp = jnp.concatenate([carry_scr[idx], x], axis=0)
        w = cw_ref[idx]
        y = w[0:1] * xp[5:5 + t_blk]
        for j in range(1, GDN_CONV):
            y = y + w[j:j + 1] * xp[5 + j:5 + j + t_blk]
        carry_scr[idx] = x[t_blk - 8:t_blk]
        return y * _sigmoid(y)

    yq = conv_silu(q_ref, 0)
    yk = conv_silu(k_ref, 1)
    vs[...] = conv_silu(v_ref, 2)
    for h in range(GDN_HEADS):
        sl = slice(h * GDN_D, (h + 1) * GDN_D)
        qh = yq[:, sl]
        kh = yk[:, sl]
        qs[:, sl] = qh * (lax.rsqrt(jnp.sum(qh * qh, axis=-1, keepdims=True) + NORM_EPS) * (GDN_D ** -0.5))
        ks[:, sl] = kh * lax.rsqrt(jnp.sum(kh * kh, axis=-1, keepdims=True) + NORM_EPS)

    g = -jnp.exp(alog_ref[...]) * _softplus(a_ref[...] + dtb_ref[...])
    beta = _sigmoid(b_ref[...])
    rin = lax.broadcasted_iota(jnp.int32, (t_blk, LANES), 0) & (c_len - 1)
    s = 1
    while s < c_len:
        g = g + jnp.where(rin >= s, pltpu.roll(g, s, axis=0), 0.0)
        s *= 2
    gcol[...] = g
    bcol[...] = beta
    g_t = g.T
    b_t = beta.T
    for c in range(n_chunk):
        cols = slice(c * c_len, (c + 1) * c_len)
        grow[c] = g_t[0:8, cols]
        brow[c] = jnp.concatenate([b_t[0:8, cols], b_t[0:8, cols]], axis=1)

    ii = lax.broadcasted_iota(jnp.int32, (c_len, c_len), 0)
    jj = lax.broadcasted_iota(jnp.int32, (c_len, c_len), 1)
    incl = ii >= jj
    strict = ii > jj
    eye = jnp.where(ii == jj, 1.0, 0.0).astype(F32)
    lane_lo = lax.broadcasted_iota(jnp.int32, (c_len, 2 * c_len), 1) < c_len
    zeros_w = jnp.zeros((c_len, 2 * c_len), BF16)
    zeros_rhs = jnp.zeros((c_len, 2 * GDN_D), BF16)
    head_sl = [slice(h * GDN_D, (h + 1) * GDN_D) for h in range(GDN_HEADS)]

    def stage_a(cp, _):
        chains = []
        for dc in range(A_CHUNKS_PER_STEP):
            c = cp * A_CHUNKS_PER_STEP + dc
            rows = pl.ds(pl.multiple_of(c * c_len, c_len), c_len)
            g_c = gcol[rows, :]
            b_c = bcol[rows, :]
            g_r = grow[c]
            b_r = brow[c]
            for h in range(GDN_HEADS):
                chains.append(dict(
                    idx=c * GDN_HEADS + h, rows=rows, sl=head_sl[h],
                    q=qs[rows, head_sl[h]], k=ks[rows, head_sl[h]], v=vs[rows, head_sl[h]],
                    gi=g_c[:, h:h + 1], gj=g_r[h:h + 1, :], bi=b_c[:, h:h + 1], bj2=b_r[h:h + 1, :],
                    g_last=g_c[c_len - 1:c_len, h:h + 1]))
        for x in chains:
            kb = x["k"].astype(BF16)
            x["qk"] = _dot_nt(jnp.concatenate([x["q"].astype(BF16), kb], axis=0), kb)
        for x in chains:
            gam = jnp.exp(jnp.where(incl, x["gi"] - x["gj"], -jnp.inf))
            x["attn"] = (x["qk"][:c_len] * gam).astype(BF16)
            a_mat = jnp.where(strict, x["bi"] * x["qk"][c_len:] * gam, 0.0)
            x["w"] = jnp.concatenate([-a_mat, eye], axis=1)
        for _ in range(6):
            for x in chains:
                wb = x["w"].astype(BF16)
                x["r"] = _dot(wb, jnp.concatenate([wb, zeros_w], axis=0))
            for x in chains:
                x["w"] = jnp.where(lane_lo, x["r"], x["w"] + x["r"])
        for x in chains:
            eg = jnp.exp(x["gi"])
            rhs = jnp.concatenate([x["v"], x["k"] * eg], axis=1).astype(BF16)
            t_mat = (x["w"] * x["bj2"]).astype(BF16)
            x["uw"] = _dot(t_mat, jnp.concatenate([zeros_rhs, rhs], axis=0))
            x["qe"] = (x["q"] * eg).astype(BF16)
            x["kd"] = (x["k"] * jnp.exp(x["g_last"] - x["gi"])).astype(BF16)
        for x in chains:
            us[x["rows"], x["sl"]] = x["uw"][:, :GDN_D]
            wqe[x["idx"], 0:c_len, :] = x["uw"][:, GDN_D:].astype(BF16)
            wqe[x["idx"], c_len:2 * c_len, :] = x["qe"]
            attn_s[x["idx"]] = x["attn"]
            kd_s[x["idx"]] = x["kd"]
        return 0

    lax.fori_loop(0, n_chunk // A_CHUNKS_PER_STEP, stage_a, 0)

    def stage_b(c, _):
        rows = pl.ds(pl.multiple_of(c * c_len, c_len), c_len)
        g_c = gcol[rows, :]
        idx = [c * GDN_HEADS + h for h in range(GDN_HEADS)]
        st = [state_scr[h] for h in range(GDN_HEADS)]
        wq = [_dot(wqe[idx[h]], st[h].astype(BF16)) for h in range(GDN_HEADS)]
        v_new = [(us[rows, head_sl[h]] - wq[h][:c_len]).astype(BF16) for h in range(GDN_HEADS)]
        o = [wq[h][c_len:] + _dot(attn_s[idx[h]], v_new[h]) for h in range(GDN_HEADS)]
        s_new = [st[h] * jnp.exp(g_c[c_len - 1:c_len, h:h + 1]) + _dot_tn(kd_s[idx[h]], v_new[h])
                 for h in range(GDN_HEADS)]
        for h in range(GDN_HEADS):
            state_scr[h] = s_new[h]
            us[rows, head_sl[h]] = o[h]
        return 0

    lax.fori_loop(0, n_chunk, stage_b, 0)

    nw = nw_ref[...]
    for h in range(GDN_HEADS):
        o = us[:, head_sl[h]]
        on = o * lax.rsqrt(jnp.mean(o * o, axis=-1, keepdims=True) + NORM_EPS) * nw
        z = z_ref[:, head_sl[h]].astype(F32)
        o_ref[:, head_sl[h]] = (on * (z * _sigmoid(z))).astype(o_ref.dtype)


def _gdn(proj, ab, cw, alog, dtb, nw, *, batch, seq, t_blk=256):
    m = batch * seq
    hw = GDN_HEADS * GDN_D
    nt = seq // t_blk
    n_chunk = t_blk // GDN_CHUNK
    tok = lambda col: pl.BlockSpec((t_blk, hw), lambda b, t, col=col: (b * nt + t, col))
    const2 = lambda shape: pl.BlockSpec(shape, lambda b, t: (0,) * len(shape))
    return pl.pallas_call(
        functools.partial(_gdn_kernel, t_blk=t_blk),
        grid=(batch, nt),
        in_specs=[
            tok(COL_GDN_Q), tok(COL_GDN_K), tok(COL_GDN_V), tok(COL_GDN_Z),
            pl.BlockSpec((t_blk, LANES), lambda b, t: (b * nt + t, 0)),
            pl.BlockSpec((t_blk, LANES), lambda b, t: (b * nt + t, 1)),
            const2((3, GDN_CONV, hw)), const2((1, LANES)), const2((1, LANES)), const2((1, GDN_D)),
        ],
        out_specs=pl.BlockSpec((t_blk, hw), lambda b, t: (b * nt + t, 0)),
        out_shape=jax.ShapeDtypeStruct((m, hw), BF16),
        scratch_shapes=[
            pltpu.VMEM((GDN_HEADS, GDN_D, GDN_D), F32),
            pltpu.VMEM((3, 8, hw), F32),
            pltpu.VMEM((t_blk, hw), F32), pltpu.VMEM((t_blk, hw), F32), pltpu.VMEM((t_blk, hw), F32),
            pltpu.VMEM((t_blk, LANES), F32), pltpu.VMEM((t_blk, LANES), F32),
            pltpu.VMEM((n_chunk, 8, GDN_CHUNK), F32), pltpu.VMEM((n_chunk, 8, 2 * GDN_CHUNK), F32),
            pltpu.VMEM((t_blk, hw), F32),
            pltpu.VMEM((n_chunk * GDN_HEADS, 2 * GDN_CHUNK, GDN_D), BF16),
            pltpu.VMEM((n_chunk * GDN_HEADS, GDN_CHUNK, GDN_CHUNK), BF16),
            pltpu.VMEM((n_chunk * GDN_HEADS, GDN_CHUNK, GDN_D), BF16),
        ],
        name="gdn",
        compiler_params=pltpu.CompilerParams(
            dimension_semantics=("arbitrary", "arbitrary"), vmem_limit_bytes=VMEM_LIMIT),
    )(proj, proj, proj, proj, ab, ab, cw, alog, dtb, nw)


def _diff_kernel(lam_ref, nw_ref, slope_ref, q_ref, k_ref, v_ref, z_ref, o_ref, vt_scr, acc_scr,
                 s0_scr, s1_scr, *, tq, tk, lam_init):
    qi = pl.program_id(2)
    assert tq == 2 * tk

    @pl.when(qi == 0)
    def _():
        vt_scr[...] = v_ref[...].astype(F32).T.astype(BF16)

    q_t = (q_ref[...].astype(F32) * (DIFF_DH ** -0.5 * LOG2E)).T
    row = lax.broadcasted_iota(jnp.int32, (2 * DIFF_DH, tq), 0)
    qq = jnp.concatenate([jnp.where(row < DIFF_DH, q_t, 0.0), jnp.where(row >= DIFF_DH, q_t, 0.0)],
                         axis=1).astype(BF16)
    slope = slope_ref[0]
    key_pos = lax.broadcasted_iota(jnp.int32, (tk, LANES), 0).astype(F32)
    lane = lax.broadcasted_iota(jnp.int32, (tk, LANES), 1)
    b_full = slope * key_pos
    b_hi = b_full.astype(BF16).astype(F32)
    b_lo = b_full - b_hi
    k_bias = jnp.where(lane < 2, b_hi, jnp.where(lane < 4, b_lo, 0.0)).astype(BF16)
    c_full = jnp.full((LANES, 2 * tq), LOG2E, F32)
    c_hi = c_full.astype(BF16).astype(F32)
    c_lo = c_full - c_hi
    row2 = lax.broadcasted_iota(jnp.int32, (LANES, 2 * tq), 0)
    q_bias = jnp.where(row2 >= 4, 0.0, jnp.where((row2 & 1) == 0, c_hi, c_lo)).astype(BF16)
    qq = jnp.concatenate([qq, q_bias], axis=0)
    slope_l2 = slope[:, 0:1] * LOG2E
    acc_scr[...] = jnp.zeros_like(acc_scr)

    def scores(kj, s_ref):
        k0 = pl.multiple_of(kj * tk, tk)
        ka = jnp.concatenate([k_ref[pl.ds(k0, tk), :], k_bias], axis=1)
        s_ref[...] = _dot(ka, qq)

    def consume(kj, s_ref, m, l, sub):
        k0 = pl.multiple_of(kj * tk, tk)
        s = s_ref[...]
        if sub is not None:
            q_pos = lax.broadcasted_iota(jnp.int32, (tk, 2 * tq), 1) & (tq - 1)
            key_rel = lax.broadcasted_iota(jnp.int32, (tk, 2 * tq), 0) + sub * tk
            s = jnp.where(key_rel <= q_pos, s, -jnp.inf)
        off = slope_l2 * (kj * tk).astype(F32)
        m_new = jnp.maximum(m, jnp.max(s, axis=0, keepdims=True) + off)
        p = jnp.exp2(s - (m_new - off))
        alpha = jnp.exp2(m - m_new)
        l_new = alpha * l + jnp.sum(p, axis=0, keepdims=True)
        acc_scr[...] = acc_scr[...] * alpha + _dot(vt_scr[:, pl.ds(k0, tk)], p.astype(BF16))
        return m_new, l_new

    def pair(j, c):
        m, l = c
        scores(2 * j + 1, s1_scr)
        m, l = consume(2 * j, s0_scr, m, l, None)
        scores(2 * j + 2, s0_scr)
        return consume(2 * j + 1, s1_scr, m, l, None)

    m0 = jnp.full((1, 2 * tq), -jnp.inf, F32)
    l0 = jnp.zeros((1, 2 * tq), F32)
    scores(0, s0_scr)
    m, l = lax.fori_loop(0, qi, pair, (m0, l0))
    scores(2 * qi + 1, s1_scr)
    m, l = consume(2 * qi, s0_scr, m, l, 0)
    m, l = consume(2 * qi + 1, s1_scr, m, l, 1)

    lv = lam_ref[...]
    lam = (jnp.exp(jnp.sum(lv[0:1] * lv[1:2], axis=-1, keepdims=True))
           - jnp.exp(jnp.sum(lv[2:3] * lv[3:4], axis=-1, keepdims=True)) + lam_init)
    acc = acc_scr[...] / l
    o = (acc[:, :tq] - lam * acc[:, tq:]).T
    on = o * lax.rsqrt(jnp.mean(o * o, axis=-1, keepdims=True) + NORM_EPS) * nw_ref[...] * (1.0 - lam_init)
    z = z_ref[...].astype(F32)
    o_ref[...] = (on * (z * _sigmoid(z))).astype(o_ref.dtype)


def _diff_attn(proj, lam_vecs, nw, *, batch, seq, lam_init, tq=512, tk=256):
    m = batch * seq
    nq = seq // tq
    slopes = 2.0 ** (-8.0 * jnp.arange(1, DIFF_HEADS + 1, dtype=F32) / DIFF_HEADS)
    slopes = jnp.broadcast_to(slopes[:, None, None], (DIFF_HEADS, 1, LANES))
    return pl.pallas_call(
        functools.partial(_diff_kernel, tq=tq, tk=tk, lam_init=lam_init),
        grid=(batch, DIFF_HEADS, nq),
        in_specs=[
            pl.BlockSpec((4, DIFF_DH), lambda b, h, i: (0, 0)),
            pl.BlockSpec((1, DIFF_DV), lambda b, h, i: (0, 0)),
            pl.BlockSpec((1, 1, LANES), lambda b, h, i: (h, 0, 0)),
            pl.BlockSpec((tq, DIFF_DV), lambda b, h, i: (b * nq + i, COL_DIFF_Q + h)),
            pl.BlockSpec((seq, DIFF_DV), lambda b, h, i: (b, COL_DIFF_K + h)),
            pl.BlockSpec((seq, DIFF_DV), lambda b, h, i: (b, COL_DIFF_V + h)),
            pl.BlockSpec((tq, DIFF_DV), lambda b, h, i: (b * nq + i, COL_DIFF_Z + h)),
        ],
        out_specs=pl.BlockSpec((tq, DIFF_DV), lambda b, h, i: (b * nq + i, h)),
        out_shape=jax.ShapeDtypeStruct((m, DIFF_HEADS * DIFF_DV), BF16),
        scratch_shapes=[pltpu.VMEM((DIFF_DV, seq), BF16), pltpu.VMEM((DIFF_DV, 2 * tq), F32),
                        pltpu.VMEM((tk, 2 * tq), F32), pltpu.VMEM((tk, 2 * tq), F32)],
        name="diffattn",
        compiler_params=pltpu.CompilerParams(
            dimension_semantics=("arbitrary", "arbitrary", "arbitrary"), vmem_limit_bytes=VMEM_LIMIT),
    )(lam_vecs, nw, slopes, proj, proj, proj, proj)


def _memkv_kernel(mem_ref, nw_ref, w_ref, o_ref):
    x = mem_ref[...]
    ms = jnp.mean(x * x, axis=-1, keepdims=True)
    hb = ((x * lax.rsqrt(ms + NORM_EPS)) * nw_ref[...]).astype(BF16)
    o_ref[...] = _dot(hb, w_ref[...]).astype(o_ref.dtype)


def _memkv(mem2d, nw, w, *, tn=1024):
    rows = mem2d.shape[0]
    width = w.shape[1]
    return pl.pallas_call(
        _memkv_kernel,
        grid=(width // tn,),
        in_specs=[
            pl.BlockSpec((rows, D_MODEL), lambda j: (0, 0)),
            pl.BlockSpec((1, D_MODEL), lambda j: (0, 0)),
            pl.BlockSpec((D_MODEL, tn), lambda j: (0, j)),
        ],
        out_specs=pl.BlockSpec((rows, tn), lambda j: (0, j)),
        out_shape=jax.ShapeDtypeStruct((rows, width), BF16),
        name="memkv",
        compiler_params=pltpu.CompilerParams(
            dimension_semantics=("arbitrary",), vmem_limit_bytes=VMEM_LIMIT),
    )(mem2d, nw, w)


def _merge_kernel(x_ref, oa_ref, ob_ref, mq_ref, mz_ref, gg_ref, gd_ref, gm_ref, mk_ref, mv_ref,
                  wg_ref, wd_ref, wm_ref, wo_ref, pw_ref, o_ref):
    heads = []
    for h in range(MEM_HEADS):
        sl = slice(h * MEM_DH, (h + 1) * MEM_DH)
        s = _dot_nt(mq_ref[:, sl], mk_ref[:, sl]) * (MEM_DH ** -0.5)
        p = jnp.exp(s - jnp.max(s, axis=-1, keepdims=True))
        p = p / jnp.sum(p, axis=-1, keepdims=True)
        heads.append(_dot(p.astype(BF16), mv_ref[:, sl]))
    mz = mz_ref[...].astype(F32)
    oc = (jnp.concatenate(heads, axis=1) * (mz * _sigmoid(mz))).astype(BF16)

    y = _sigmoid(gg_ref[...].astype(F32)) * _dot(oa_ref[...], wg_ref[...])
    y = y + _sigmoid(gd_ref[...].astype(F32)) * _dot(ob_ref[...], wd_ref[...])
    y = y + _sigmoid(gm_ref[...].astype(F32)) * _dot(oc, wm_ref[...])
    out = _dot(y.astype(BF16), wo_ref[...])
    ms = jnp.mean(out * out, axis=-1, keepdims=True)
    o_ref[...] = x_ref[...] + (out * lax.rsqrt(ms + NORM_EPS)) * pw_ref[...]


def _merge(x, proj, oa, ob, memkv, wg, wd, wm, wo, pw, *, seq, mem_len, tm=512):
    m = x.shape[0]
    per_b = seq // tm
    mem_w = MEM_HEADS * MEM_DH
    tok = lambda width, col: pl.BlockSpec((tm, width), lambda i, col=col: (i, col))
    wspec = pl.BlockSpec((D_MODEL, D_MODEL), lambda i: (0, 0))
    return pl.pallas_call(
        _merge_kernel,
        grid=(m // tm,),
        in_specs=[
            tok(D_MODEL, 0), tok(D_MODEL, 0), tok(D_MODEL, 0),
            tok(mem_w, COL_MEM_Q), tok(mem_w, COL_MEM_Z),
            tok(D_MODEL, COL_GATES), tok(D_MODEL, COL_GATES + 1), tok(D_MODEL, COL_GATES + 2),
            pl.BlockSpec((mem_len, mem_w), lambda i: (i // per_b, 0)),
            pl.BlockSpec((mem_len, mem_w), lambda i: (i // per_b, 1)),
            wspec, wspec, wspec, wspec,
            pl.BlockSpec((1, D_MODEL), lambda i: (0, 0)),
        ],
        out_specs=pl.BlockSpec((tm, D_MODEL), lambda i: (i, 0)),
        out_shape=jax.ShapeDtypeStruct((m, D_MODEL), F32),
        name="merge",
        compiler_params=pltpu.CompilerParams(
            dimension_semantics=("arbitrary",), vmem_limit_bytes=VMEM_LIMIT),
    )(x, oa, ob, proj, proj, proj, proj, proj, memkv, memkv, wg, wd, wm, wo, pw)


def kernel(x, mem, pre_norm_w, post_norm_w, w_in, gdn_conv_w, gdn_a_log, gdn_dt_bias, gdn_norm_w,
           diff_lambda, diff_norm_w, mem_norm_w, w_mem_kv, w_br_gdn, w_br_diff, w_br_mem, w_out):
    batch, seq, d = x.shape
    mem_len = mem.shape[1]
    depth = w_in.shape[0]
    assert d == D_MODEL and seq % 512 == 0

    w_main = jnp.concatenate([w_in[:, :, :QKV_W], w_in[:, :, QKV_W + 2 * GDN_HEADS:]], axis=-1).astype(BF16)
    w_a = w_in[:, :, QKV_W:QKV_W + GDN_HEADS]
    w_b = w_in[:, :, QKV_W + GDN_HEADS:QKV_W + 2 * GDN_HEADS]
    pad = jnp.zeros((depth, D_MODEL, LANES - GDN_HEADS), w_in.dtype)
    w_ab = jnp.concatenate([w_a, pad, w_b, pad], axis=-1).astype(BF16)
    hw = GDN_HEADS * GDN_D
    conv_w = gdn_conv_w.reshape(depth, GDN_CONV, 3, hw).transpose(0, 2, 1, 3)
    lane_pad = lambda t: jnp.pad(t, ((0, 0), (0, LANES - GDN_HEADS)))[:, None, :]
    a_log = lane_pad(gdn_a_log)
    dt_bias = lane_pad(gdn_dt_bias)
    w_mem_kv_b = w_mem_kv.astype(BF16)
    w_g, w_d, w_m, w_o = (t.astype(BF16) for t in (w_br_gdn, w_br_diff, w_br_mem, w_out))

    xf = x.reshape(batch * seq, d)
    mem2d = mem.reshape(batch * mem_len, d)
    for l in range(depth):
        lam_init = 0.8 - 0.6 * math.exp(-0.3 * l)
        proj, ab = _inproj(xf, pre_norm_w[l][None], w_main[l], w_ab[l])
        oa = _gdn(proj, ab, conv_w[l], a_log[l], dt_bias[l], gdn_norm_w[l][None], batch=batch, seq=seq)
        ob = _diff_attn(proj, diff_lambda[l], diff_norm_w[l][None], batch=batch, seq=seq, lam_init=lam_init)
        memkv = _memkv(mem2d, mem_norm_w[l][None], w_mem_kv_b[l])
        xf = _merge(xf, proj, oa, ob, memkv, w_g[l], w_d[l], w_m[l], w_o[l], post_norm_w[l][None],
                    seq=seq, mem_len=mem_len)
    return xf.reshape(batch, seq, d)
```

```python
import functools
import math

import jax
import jax.numpy as jnp
from jax import lax
from jax.experimental import pallas as pl
from jax.experimental.pallas import tpu as pltpu

F32 = jnp.float32
BF16 = jnp.bfloat16

D_MODEL = 1024
NORM_EPS = 1e-6
GDN_HEADS = 8
GDN_D = 128
GDN_CONV = 4
GDN_CHUNK = 64
A_CHUNKS_PER_STEP = 2
DIFF_HEADS = 8
DIFF_DH = 64
DIFF_DV = 2 * DIFF_DH
MEM_HEADS = 4
MEM_DH = 256
N_BRANCH = 3

LANES = 128
LOG2E = 1.4426950408889634
ONES_ROWS = 16
QKV_W = 3 * GDN_HEADS * GDN_D
COL_GDN_Q, COL_GDN_K, COL_GDN_V, COL_GDN_Z = 0, 1, 2, 3
COL_DIFF_Q, COL_DIFF_K, COL_DIFF_V, COL_DIFF_Z = 32, 40, 48, 56
COL_MEM_Q, COL_MEM_Z = 8, 9
COL_GATES = 10
PROJ_W = 13 * D_MODEL
AB_W = 2 * LANES

VMEM_LIMIT = 56 * 1024 * 1024


def _sigmoid(x):
    return 1.0 / (1.0 + jnp.exp(-x))


def _softplus(x):
    return jnp.maximum(x, 0.0) + jnp.log(1.0 + jnp.exp(-jnp.abs(x)))


def _dot(a, b):
    return jnp.dot(a, b, preferred_element_type=F32)


def _dot_nt(a, b):
    return lax.dot_general(a, b, (((1,), (1,)), ((), ())), preferred_element_type=F32)


def _dot_tn(a, b):
    return lax.dot_general(a, b, (((0,), (0,)), ((), ())), preferred_element_type=F32)


def _inproj_kernel(x_ref, nw_ref, w_ref, wab_ref, o_ref, oab_ref, h_scr):
    @pl.when(pl.program_id(1) == 0)
    def _():
        x = x_ref[...]
        ms = jnp.mean(x * x, axis=-1, keepdims=True)
        hb = ((x * lax.rsqrt(ms + NORM_EPS)) * nw_ref[...]).astype(BF16)
        h_scr[...] = hb
        oab_ref[...] = _dot(hb, wab_ref[...])

    o_ref[...] = _dot(h_scr[...], w_ref[...]).astype(o_ref.dtype)


def _inproj(x, nw, w, wab, *, tm=2048, tn=1024):
    m = x.shape[0]
    tm = min(tm, m)
    return pl.pallas_call(
        _inproj_kernel,
        grid=(m // tm, PROJ_W // tn),
        in_specs=[
            pl.BlockSpec((tm, D_MODEL), lambda i, j: (i, 0)),
            pl.BlockSpec((1, D_MODEL), lambda i, j: (0, 0)),
            pl.BlockSpec((D_MODEL, tn), lambda i, j: (0, j)),
            pl.BlockSpec((D_MODEL, AB_W), lambda i, j: (0, 0)),
        ],
        out_specs=[
            pl.BlockSpec((tm, tn), lambda i, j: (i, j)),
            pl.BlockSpec((tm, AB_W), lambda i, j: (i, 0)),
        ],
        out_shape=[
            jax.ShapeDtypeStruct((m, PROJ_W), BF16),
            jax.ShapeDtypeStruct((m, AB_W), F32),
        ],
        scratch_shapes=[pltpu.VMEM((tm, D_MODEL), BF16)],
        name="inproj",
        compiler_params=pltpu.CompilerParams(
            dimension_semantics=("arbitrary", "arbitrary"), vmem_limit_bytes=VMEM_LIMIT),
    )(x, nw, w, wab)


def _gdn_kernel(q_ref, k_ref, v_ref, z_ref, a_ref, b_ref, cw_ref, alog_ref, dtb_ref, nw_ref, o_ref,
                state_scr, xpad, qs, ks, vs, gcol, bcol, grow, brow, us, wqe, attn_s, kd_s, *, t_blk):
    c_len = GDN_CHUNK
    n_chunk = t_blk // c_len
    hw = GDN_HEADS * GDN_D

    @pl.when(pl.program_id(1) == 0)
    def _():
        state_scr[...] = jnp.zeros_like(state_scr)
        xpad[:, 0:8, :] = jnp.zeros((3, 8, hw), F32)

    def conv_silu(ref, idx):
        xpad[idx, 8:8 + t_blk, :] = ref[...].astype(F32)
        w = cw_ref[idx]
        y = w[GDN_CONV - 1:GDN_CONV] * xpad[idx, 8:8 + t_blk, :]
        for j in range(GDN_CONV - 1):
            y = y + w[j:j + 1] * xpad[idx, 5 + j:5 + j + t_blk, :]
        xpad[idx, 0:8, :] = xpad[idx, t_blk:t_blk + 8, :]
        return y * _sigmoid(y)

    yq = conv_silu(q_ref, 0)
    yk = conv_silu(k_ref, 1)
    vs[...] = conv_silu(v_ref, 2)
    for h in range(GDN_HEADS):
        sl = slice(h * GDN_D, (h + 1) * GDN_D)
        qh = yq[:, sl]
        kh = yk[:, sl]
        qs[:, sl] = qh * (lax.rsqrt(jnp.sum(qh * qh, axis=-1, keepdims=True) + NORM_EPS) * (GDN_D ** -0.5))
        ks[:, sl] = kh * lax.rsqrt(jnp.sum(kh * kh, axis=-1, keepdims=True) + NORM_EPS)

    g = -jnp.exp(alog_ref[...]) * _softplus(a_ref[...] + dtb_ref[...])
    beta = _sigmoid(b_ref[...])
    rin = lax.broadcasted_iota(jnp.int32, (t_blk, LANES), 0) & (c_len - 1)
    s = 1
    while s < c_len:
        g = g + jnp.where(rin >= s, pltpu.roll(g, s, axis=0), 0.0)
        s *= 2
    gcol[...] = g
    bcol[...] = beta
    g_t = g.T
    b_t = beta.T
    for c in range(n_chunk):
        cols = slice(c * c_len, (c + 1) * c_len)
        grow[c] = g_t[0:8, cols]
        brow[c] = jnp.concatenate([b_t[0:8, cols], b_t[0:8, cols]], axis=1)

    ii = lax.broadcasted_iota(jnp.int32, (c_len, c_len), 0)
    jj = lax.broadcasted_iota(jnp.int32, (c_len, c_len), 1)
    incl = ii >= jj
    strict = ii > jj
    eye = jnp.where(ii == jj, 1.0, 0.0).astype(F32)
    lane_lo = lax.broadcasted_iota(jnp.int32, (c_len, 2 * c_len), 1) < c_len
    zeros_w = jnp.zeros((c_len, 2 * c_len), BF16)
    zeros_rhs = jnp.zeros((c_len, 2 * GDN_D), BF16)
    head_sl = [slice(h * GDN_D, (h + 1) * GDN_D) for h in range(GDN_HEADS)]

    def stage_a(cp, _):
        chains = []
        for dc in range(A_CHUNKS_PER_STEP):
            c = cp * A_CHUNKS_PER_STEP + dc
            rows = pl.ds(pl.multiple_of(c * c_len, c_len), c_len)
            g_c = gcol[rows, :]
            b_c = bcol[rows, :]
            g_r = grow[c]
            b_r = brow[c]
            for h in range(GDN_HEADS):
                chains.append(dict(
                    idx=c * GDN_HEADS + h, rows=rows, sl=head_sl[h],
                    q=qs[rows, head_sl[h]], k=ks[rows, head_sl[h]], v=vs[rows, head_sl[h]],
                    gi=g_c[:, h:h + 1], gj=g_r[h:h + 1, :], bi=b_c[:, h:h + 1], bj2=b_r[h:h + 1, :],
                    g_last=g_c[c_len - 1:c_len, h:h + 1]))
        for x in chains:
            kb = x["k"].astype(BF16)
            x["qk"] = _dot_nt(jnp.concatenate([x["q"].astype(BF16), kb], axis=0), kb)
        for x in chains:
            gam = jnp.exp(jnp.where(incl, x["gi"] - x["gj"], -jnp.inf))
            x["attn"] = (x["qk"][:c_len] * gam).astype(BF16)
            a_mat = jnp.where(strict, x["bi"] * x["qk"][c_len:] * gam, 0.0)
            x["w"] = jnp.concatenate([-a_mat, eye], axis=1)
        for _ in range(6):
            for x in chains:
                wb = x["w"].astype(BF16)
                x["r"] = _dot(wb, jnp.concatenate([wb, zeros_w], axis=0))
            for x in chains:
                x["w"] = jnp.where(lane_lo, x["r"], x["w"] + x["r"])
        for x in chains:
            eg = jnp.exp(x["gi"])
            rhs = jnp.concatenate([x["v"], x["k"] * eg], axis=1).astype(BF16)
            t_mat = (x["w"] * x["bj2"]).astype(BF16)
            x["uw"] = _dot(t_mat, jnp.concatenate([zeros_rhs, rhs], axis=0))
            x["qe"] = (x["q"] * eg).astype(BF16)
            x["kd"] = (x["k"] * jnp.exp(x["g_last"] - x["gi"])).astype(BF16)
        for x in chains:
            us[x["rows"], x["sl"]] = x["uw"][:, :GDN_D]
            wqe[x["idx"], 0:c_len, :] = x["uw"][:, GDN_D:].astype(BF16)
            wqe[x["idx"], c_len:2 * c_len, :] = x["qe"]
            attn_s[x["idx"]] = x["attn"]
            kd_s[x["idx"]] = x["kd"]
        return 0

    lax.fori_loop(0, n_chunk // A_CHUNKS_PER_STEP, stage_a, 0)

    def stage_b(c, _):
        rows = pl.ds(pl.multiple_of(c * c_len, c_len), c_len)
        g_c = gcol[rows, :]
        idx = [c * GDN_HEADS + h for h in range(GDN_HEADS)]
        st = [state_scr[h] for h in range(GDN_HEADS)]
        wq = [_dot(wqe[idx[h]], st[h].astype(BF16)) for h in range(GDN_HEADS)]
        v_new = [(us[rows, head_sl[h]] - wq[h][:c_len]).astype(BF16) for h in range(GDN_HEADS)]
        o = [wq[h][c_len:] + _dot(attn_s[idx[h]], v_new[h]) for h in range(GDN_HEADS)]
        s_new = [st[h] * jnp.exp(g_c[c_len - 1:c_len, h:h + 1]) + _dot_tn(kd_s[idx[h]], v_new[h])
                 for h in range(GDN_HEADS)]
        for h in range(GDN_HEADS):
            state_scr[h] = s_new[h]
            us[rows, head_sl[h]] = o[h]
        return 0

    lax.fori_loop(0, n_chunk, stage_b, 0)

    nw = nw_ref[...]
    for h in range(GDN_HEADS):
        o = us[:, head_sl[h]]
        on = o * lax.rsqrt(jnp.mean(o * o, axis=-1, keepdims=True) + NORM_EPS) * nw
        z = z_ref[:, head_sl[h]].astype(F32)
        o_ref[:, head_sl[h]] = (on * (z * _sigmoid(z))).astype(o_ref.dtype)


def _gdn(proj, ab, cw, alog, dtb, nw, *, batch, seq, t_blk=256):
    m = batch * seq
    hw = GDN_HEADS * GDN_D
    nt = seq // t_blk
    n_chunk = t_blk // GDN_CHUNK
    tok = lambda col: pl.BlockSpec((t_blk, hw), lambda b, t, col=col: (b * nt + t, col))
    const2 = lambda shape: pl.BlockSpec(shape, lambda b, t: (0,) * len(shape))
    return pl.pallas_call(
        functools.partial(_gdn_kernel, t_blk=t_blk),
        grid=(batch, nt),
        in_specs=[
            tok(COL_GDN_Q), tok(COL_GDN_K), tok(COL_GDN_V), tok(COL_GDN_Z),
            pl.BlockSpec((t_blk, LANES), lambda b, t: (b * nt + t, 0)),
            pl.BlockSpec((t_blk, LANES), lambda b, t: (b * nt + t, 1)),
            const2((3, GDN_CONV, hw)), const2((1, LANES)), const2((1, LANES)), const2((1, GDN_D)),
        ],
        out_specs=pl.BlockSpec((t_blk, hw), lambda b, t: (b * nt + t, 0)),
        out_shape=jax.ShapeDtypeStruct((m, hw), BF16),
        scratch_shapes=[
            pltpu.VMEM((GDN_HEADS, GDN_D, GDN_D), F32),
            pltpu.VMEM((3, 8 + t_blk, hw), F32),
            pltpu.VMEM((t_blk, hw), F32), pltpu.VMEM((t_blk, hw), F32), pltpu.VMEM((t_blk, hw), F32),
            pltpu.VMEM((t_blk, LANES), F32), pltpu.VMEM((t_blk, LANES), F32),
            pltpu.VMEM((n_chunk, 8, GDN_CHUNK), F32), pltpu.VMEM((n_chunk, 8, 2 * GDN_CHUNK), F32),
            pltpu.VMEM((t_blk, hw), F32),
            pltpu.VMEM((n_chunk * GDN_HEADS, 2 * GDN_CHUNK, GDN_D), BF16),
            pltpu.VMEM((n_chunk * GDN_HEADS, GDN_CHUNK, GDN_CHUNK), BF16),
            pltpu.VMEM((n_chunk * GDN_HEADS, GDN_CHUNK, GDN_D), BF16),
        ],
        name="gdn",
        compiler_params=pltpu.CompilerParams(
            dimension_semantics=("arbitrary", "arbitrary"), vmem_limit_bytes=VMEM_LIMIT),
    )(proj, proj, proj, proj, ab, ab, cw, alog, dtb, nw)


def _diff_kernel(lam_ref, nw_ref, slope_ref, q_ref, k_ref, v_ref, z_ref, o_ref, vt_scr, acc_scr,
                 s0_scr, s1_scr, *, tq, tk, lam_init):
    qi = pl.program_id(2)
    n_sub = tq // tk
    assert n_sub * tk == tq and n_sub % 2 == 0

    @pl.when(qi == 0)
    def _():
        vt_scr[0:DIFF_DV, :] = v_ref[...].astype(F32).T.astype(BF16)
        vt_scr[DIFF_DV:, :] = jnp.ones((ONES_ROWS, vt_scr.shape[1]), BF16)

    q_t = (q_ref[...].astype(F32) * (DIFF_DH ** -0.5 * LOG2E)).T
    row = lax.broadcasted_iota(jnp.int32, (2 * DIFF_DH, tq), 0)
    qq = jnp.concatenate([jnp.where(row < DIFF_DH, q_t, 0.0), jnp.where(row >= DIFF_DH, q_t, 0.0)],
                         axis=1).astype(BF16)
    slope = slope_ref[0]
    key_pos = lax.broadcasted_iota(jnp.int32, (tk, LANES), 0).astype(F32)
    lane = lax.broadcasted_iota(jnp.int32, (tk, LANES), 1)
    b_full = slope * key_pos
    b_hi = b_full.astype(BF16).astype(F32)
    b_lo = b_full - b_hi
    k_bias = jnp.where(lane < 2, b_hi, jnp.where(lane < 4, b_lo, 0.0)).astype(BF16)
    c_full = jnp.full((LANES, 2 * tq), LOG2E, F32)
    c_hi = c_full.astype(BF16).astype(F32)
    c_lo = c_full - c_hi
    row2 = lax.broadcasted_iota(jnp.int32, (LANES, 2 * tq), 0)
    q_bias = jnp.where(row2 >= 4, 0.0, jnp.where((row2 & 1) == 0, c_hi, c_lo)).astype(BF16)
    qq = jnp.concatenate([qq, q_bias], axis=0)
    slope_l2 = slope[:, 0:1] * LOG2E
    acc_scr[...] = jnp.zeros_like(acc_scr)

    slabs = [(c, slice(c * tk, (c + 1) * tk), c % n_sub) for c in range(2 * n_sub)]

    def scores(kj, s_ref, sub):
        k0 = pl.multiple_of(kj * tk, tk)
        ka = jnp.concatenate([k_ref[pl.ds(k0, tk), :], k_bias], axis=1)
        if sub is None:
            s_ref[...] = _dot(ka, qq)
        else:
            for _, cs, pos in slabs:
                if pos >= sub:
                    s_ref[:, cs] = _dot(ka, qq[:, cs])

    def consume(kj, s_ref, m, sub):
        k0 = pl.multiple_of(kj * tk, tk)
        off = slope_l2 * (kj * tk).astype(F32)
        vt = vt_scr[:, pl.ds(k0, tk)]
        m_out = []
        for _, cs, pos in slabs:
            if sub is not None and pos < sub:
                m_out.append(m[:, cs])
                continue
            s = s_ref[:, cs]
            if sub is not None and pos == sub:
                q_pos = lax.broadcasted_iota(jnp.int32, (tk, tk), 1)
                key_pos_blk = lax.broadcasted_iota(jnp.int32, (tk, tk), 0)
                s = jnp.where(key_pos_blk <= q_pos, s, -jnp.inf)
            m_new = jnp.maximum(m[:, cs], jnp.max(s, axis=0, keepdims=True) + off)
            p = jnp.exp2(s - (m_new - off))
            alpha = jnp.exp2(m[:, cs] - m_new)
            m_out.append(m_new)
            acc_scr[:, cs] = acc_scr[:, cs] * alpha + _dot(vt, p.astype(BF16))
        return jnp.concatenate(m_out, axis=1)

    def pair(j, m):
        scores(2 * j + 1, s1_scr, None)
        m = consume(2 * j, s0_scr, m, None)
        scores(2 * j + 2, s0_scr, None)
        return consume(2 * j + 1, s1_scr, m, None)

    m0 = jnp.full((1, 2 * tq), -jnp.inf, F32)
    scores(0, s0_scr, None)
    n_full = qi * n_sub
    m = lax.fori_loop(0, n_full // 2, pair, m0)
    bufs = (s0_scr, s1_scr)
    for sub in range(n_sub):
        if sub + 1 < n_sub:
            scores(n_full + sub + 1, bufs[(sub + 1) % 2], sub + 1)
        m = consume(n_full + sub, bufs[sub % 2], m, sub)

    lv = lam_ref[...]
    lam = (jnp.exp(jnp.sum(lv[0:1] * lv[1:2], axis=-1, keepdims=True))
           - jnp.exp(jnp.sum(lv[2:3] * lv[3:4], axis=-1, keepdims=True)) + lam_init)
    acc = acc_scr[0:DIFF_DV, :] / acc_scr[DIFF_DV:DIFF_DV + 1, :]
    o = (acc[:, :tq] - lam * acc[:, tq:]).T
    on = o * lax.rsqrt(jnp.mean(o * o, axis=-1, keepdims=True) + NORM_EPS) * nw_ref[...] * (1.0 - lam_init)
    z = z_ref[...].astype(F32)
    o_ref[...] = (on * (z * _sigmoid(z))).astype(o_ref.dtype)


def _diff_attn(proj, lam_vecs, nw, *, batch, seq, lam_init, tq=1024, tk=256):
    m = batch * seq
    nq = seq // tq
    slopes = 2.0 ** (-8.0 * jnp.arange(1, DIFF_HEADS + 1, dtype=F32) / DIFF_HEADS)
    slopes = jnp.broadcast_to(slopes[:, None, None], (DIFF_HEADS, 1, LANES))
    return pl.pallas_call(
        functools.partial(_diff_kernel, tq=tq, tk=tk, lam_init=lam_init),
        grid=(batch, DIFF_HEADS, nq),
        in_specs=[
            pl.BlockSpec((4, DIFF_DH), lambda b, h, i: (0, 0)),
            pl.BlockSpec((1, DIFF_DV), lambda b, h, i: (0, 0)),
            pl.BlockSpec((1, 1, LANES), lambda b, h, i: (h, 0, 0)),
            pl.BlockSpec((tq, DIFF_DV), lambda b, h, i: (b * nq + i, COL_DIFF_Q + h)),
            pl.BlockSpec((seq, DIFF_DV), lambda b, h, i: (b, COL_DIFF_K + h)),
            pl.BlockSpec((seq, DIFF_DV), lambda b, h, i: (b, COL_DIFF_V + h)),
            pl.BlockSpec((tq, DIFF_DV), lambda b, h, i: (b * nq + i, COL_DIFF_Z + h)),
        ],
        out_specs=pl.BlockSpec((tq, DIFF_DV), lambda b, h, i: (b * nq + i, h)),
        out_shape=jax.ShapeDtypeStruct((m, DIFF_HEADS * DIFF_DV), BF16),
        scratch_shapes=[pltpu.VMEM((DIFF_DV + ONES_ROWS, seq), BF16),
                        pltpu.VMEM((DIFF_DV + ONES_ROWS, 2 * tq), F32),
                        pltpu.VMEM((tk, 2 * tq), F32), pltpu.VMEM((tk, 2 * tq), F32)],
        name="diffattn",
        compiler_params=pltpu.CompilerParams(
            dimension_semantics=("arbitrary", "arbitrary", "arbitrary"), vmem_limit_bytes=VMEM_LIMIT),
    )(lam_vecs, nw, slopes, proj, proj, proj, proj)


def _memkv_kernel(mem_ref, nw_ref, w_ref, o_ref):
    x = mem_ref[...]
    ms = jnp.mean(x * x, axis=-1, keepdims=True)
    hb = ((x * lax.rsqrt(ms + NORM_EPS)) * nw_ref[...]).astype(BF16)
    o_ref[...] = _dot(hb, w_ref[...]).astype(o_ref.dtype)


def _memkv(mem2d, nw, w, *, tn=1024):
    rows = mem2d.shape[0]
    width = w.shape[1]
    return pl.pallas_call(
        _memkv_kernel,
        grid=(width // tn,),
        in_specs=[
            pl.BlockSpec((rows, D_MODEL), lambda j: (0, 0)),
            pl.BlockSpec((1, D_MODEL), lambda j: (0, 0)),
            pl.BlockSpec((D_MODEL, tn), lambda j: (0, j)),
        ],
        out_specs=pl.BlockSpec((rows, tn), lambda j: (0, j)),
        out_shape=jax.ShapeDtypeStruct((rows, width), BF16),
        name="memkv",
        compiler_params=pltpu.CompilerParams(
            dimension_semantics=("arbitrary",), vmem_limit_bytes=VMEM_LIMIT),
    )(mem2d, nw, w)


def _merge_kernel(x_ref, oa_ref, ob_ref, mq_ref, mz_ref, gg_ref, gd_ref, gm_ref, mk_ref, mv_ref,
                  wg_ref, wd_ref, wm_ref, wo_ref, pw_ref, o_ref):
    heads = []
    for h in range(MEM_HEADS):
        sl = slice(h * MEM_DH, (h + 1) * MEM_DH)
        s = _dot_nt(mq_ref[:, sl], mk_ref[:, sl]) * (MEM_DH ** -0.5)
        p = jnp.exp(s - jnp.max(s, axis=-1, keepdims=True))
        p = p / jnp.sum(p, axis=-1, keepdims=True)
        heads.append(_dot(p.astype(BF16), mv_ref[:, sl]))
    mz = mz_ref[...].astype(F32)
    oc = (jnp.concatenate(heads, axis=1) * (mz * _sigmoid(mz))).astype(BF16)

    y = _sigmoid(gg_ref[...].astype(F32)) * _dot(oa_ref[...], wg_ref[...])
    y = y + _sigmoid(gd_ref[...].astype(F32)) * _dot(ob_ref[...], wd_ref[...])
    y = y + _sigmoid(gm_ref[...].astype(F32)) * _dot(oc, wm_ref[...])
    out = _dot(y.astype(BF16), wo_ref[...])
    ms = jnp.mean(out * out, axis=-1, keepdims=True)
    o_ref[...] = x_ref[...] + (out * lax.rsqrt(ms + NORM_EPS)) * pw_ref[...]


def _merge(x, proj, oa, ob, memkv, wg, wd, wm, wo, pw, *, seq, mem_len, tm=512):
    m = x.shape[0]
    per_b = seq // tm
    mem_w = MEM_HEADS * MEM_DH
    tok = lambda width, col: pl.BlockSpec((tm, width), lambda i, col=col: (i, col))
    wspec = pl.BlockSpec((D_MODEL, D_MODEL), lambda i: (0, 0))
    return pl.pallas_call(
        _merge_kernel,
        grid=(m // tm,),
        in_specs=[
            tok(D_MODEL, 0), tok(D_MODEL, 0), tok(D_MODEL, 0),
            tok(mem_w, COL_MEM_Q), tok(mem_w, COL_MEM_Z),
            tok(D_MODEL, COL_GATES), tok(D_MODEL, COL_GATES + 1), tok(D_MODEL, COL_GATES + 2),
            pl.BlockSpec((mem_len, mem_w), lambda i: (i // per_b, 0)),
            pl.BlockSpec((mem_len, mem_w), lambda i: (i // per_b, 1)),
            wspec, wspec, wspec, wspec,
            pl.BlockSpec((1, D_MODEL), lambda i: (0, 0)),
        ],
        out_specs=pl.BlockSpec((tm, D_MODEL), lambda i: (i, 0)),
        out_shape=jax.ShapeDtypeStruct((m, D_MODEL), F32),
        name="merge",
        compiler_params=pltpu.CompilerParams(
            dimension_semantics=("arbitrary",), vmem_limit_bytes=VMEM_LIMIT),
    )(x, oa, ob, proj, proj, proj, proj, proj, memkv, memkv, wg, wd, wm, wo, pw)


def kernel(x, mem, pre_norm_w, post_norm_w, w_in, gdn_conv_w, gdn_a_log, gdn_dt_bias, gdn_norm_w,
           diff_lambda, diff_norm_w, mem_norm_w, w_mem_kv, w_br_gdn, w_br_diff, w_br_mem, w_out):
    batch, seq, d = x.shape
    mem_len = mem.shape[1]
    depth = w_in.shape[0]
    assert d == D_MODEL and seq % 1024 == 0

    w_main = jnp.concatenate([w_in[:, :, :QKV_W], w_in[:, :, QKV_W + 2 * GDN_HEADS:]], axis=-1).astype(BF16)
    w_a = w_in[:, :, QKV_W:QKV_W + GDN_HEADS]
    w_b = w_in[:, :, QKV_W + GDN_HEADS:QKV_W + 2 * GDN_HEADS]
    pad = jnp.zeros((depth, D_MODEL, LANES - GDN_HEADS), w_in.dtype)
    w_ab = jnp.concatenate([w_a, pad, w_b, pad], axis=-1).astype(BF16)
    hw = GDN_HEADS * GDN_D
    conv_w = gdn_conv_w.reshape(depth, GDN_CONV, 3, hw).transpose(0, 2, 1, 3)
    lane_pad = lambda t: jnp.pad(t, ((0, 0), (0, LANES - GDN_HEADS)))[:, None, :]
    a_log = lane_pad(gdn_a_log)
    dt_bias = lane_pad(gdn_dt_bias)
    w_mem_kv_b = w_mem_kv.astype(BF16)
    w_g, w_d, w_m, w_o = (t.astype(BF16) for t in (w_br_gdn, w_br_diff, w_br_mem, w_out))

    xf = x.reshape(batch * seq, d)
    mem2d = mem.reshape(batch * mem_len, d)
    for l in range(depth):
        lam_init = 0.8 - 0.6 * math.exp(-0.3 * l)
        proj, ab = _inproj(xf, pre_norm_w[l][None], w_main[l], w_ab[l])
        oa = _gdn(proj, ab, conv_w[l], a_log[l], dt_bias[l], gdn_norm_w[l][None], batch=batch, seq=seq)
        ob = _diff_attn(proj, diff_lambda[l], diff_norm_w[l][None], batch=batch, seq=seq, lam_init=lam_init)
        memkv = _memkv(mem2d, mem_norm_w[l][None], w_mem_kv_b[l])
        xf = _merge(xf, proj, oa, ob, memkv, w_g[l], w_d[l], w_m[l], w_o[l], post_norm_w[l][None],
                    seq=seq, mem_len=mem_len)
    return xf.reshape(batch, seq, d)
```

```python
import functools
import math

import jax
import jax.numpy as jnp
from jax import lax
from jax.experimental import pallas as pl
from jax.experimental.pallas import tpu as pltpu

F32 = jnp.float32
BF16 = jnp.bfloat16

D_MODEL = 1024
NORM_EPS = 1e-6
GDN_HEADS = 8
GDN_D = 128
GDN_CONV = 4
GDN_CHUNK = 64
DIFF_HEADS = 8
DIFF_DH = 64
DIFF_DV = 2 * DIFF_DH
MEM_HEADS = 4
MEM_DH = 256
N_BRANCH = 3

LANES = 128
LOG2E = 1.4426950408889634
ONES_ROWS = 16
LOOP_BLOCKS = 4
QKV_W = 3 * GDN_HEADS * GDN_D
COL_GDN_Q, COL_GDN_K, COL_GDN_V, COL_GDN_Z = 0, 1, 2, 3
COL_DIFF_Q, COL_DIFF_K, COL_DIFF_V, COL_DIFF_Z = 32, 40, 48, 56
COL_MEM_Q, COL_MEM_Z = 8, 9
COL_GATES = 10
PROJ_W = 13 * D_MODEL
AB_W = 2 * LANES

VMEM_LIMIT = 56 * 1024 * 1024


def _sigmoid(x):
    return 1.0 / (1.0 + jnp.exp(-x))


def _softplus(x):
    return jnp.maximum(x, 0.0) + jnp.log(1.0 + jnp.exp(-jnp.abs(x)))


def _dot(a, b):
    return jnp.dot(a, b, preferred_element_type=F32)


def _dot_nt(a, b):
    return lax.dot_general(a, b, (((1,), (1,)), ((), ())), preferred_element_type=F32)


def _dot_tn(a, b):
    return lax.dot_general(a, b, (((0,), (0,)), ((), ())), preferred_element_type=F32)


def _inproj_kernel(x_ref, nw_ref, w_ref, wab_ref, o_ref, oab_ref, h_scr):
    @pl.when(pl.program_id(1) == 0)
    def _():
        x = x_ref[...]
        ms = jnp.mean(x * x, axis=-1, keepdims=True)
        hb = ((x * lax.rsqrt(ms + NORM_EPS)) * nw_ref[...]).astype(BF16)
        h_scr[...] = hb
        oab_ref[...] = _dot(hb, wab_ref[...])

    o_ref[...] = _dot(h_scr[...], w_ref[...]).astype(o_ref.dtype)


def _inproj(x, nw, w, wab, *, tm=2048, tn=1024):
    m = x.shape[0]
    tm = min(tm, m)
    return pl.pallas_call(
        _inproj_kernel,
        grid=(m // tm, PROJ_W // tn),
        in_specs=[
            pl.BlockSpec((tm, D_MODEL), lambda i, j: (i, 0)),
            pl.BlockSpec((1, D_MODEL), lambda i, j: (0, 0)),
            pl.BlockSpec((D_MODEL, tn), lambda i, j: (0, j)),
            pl.BlockSpec((D_MODEL, AB_W), lambda i, j: (0, 0)),
        ],
        out_specs=[
            pl.BlockSpec((tm, tn), lambda i, j: (i, j)),
            pl.BlockSpec((tm, AB_W), lambda i, j: (i, 0)),
        ],
        out_shape=[
            jax.ShapeDtypeStruct((m, PROJ_W), BF16),
            jax.ShapeDtypeStruct((m, AB_W), F32),
        ],
        scratch_shapes=[pltpu.VMEM((tm, D_MODEL), BF16)],
        name="inproj",
        compiler_params=pltpu.CompilerParams(
            dimension_semantics=("arbitrary", "arbitrary"), vmem_limit_bytes=VMEM_LIMIT),
    )(x, nw, w, wab)


def _gdn_kernel(q_ref, k_ref, v_ref, z_ref, a_ref, b_ref, cw_ref, alog_ref, dtb_ref, nw_ref, o_ref,
                state_scr, xpad, qs, ks, vs, gcol, bcol, grow, brow, us, wqe, attn_s, kd_s, *, t_blk):
    c_len = GDN_CHUNK
    n_chunk = t_blk // c_len
    hw = GDN_HEADS * GDN_D

    @pl.when(pl.program_id(1) == 0)
    def _():
        state_scr[...] = jnp.zeros_like(state_scr)
        xpad[:, 0:8, :] = jnp.zeros((3, 8, hw), F32)

    def conv_silu(ref, idx):
        xpad[idx, 8:8 + t_blk, :] = ref[...].astype(F32)
        w = cw_ref[idx]
        y = w[GDN_CONV - 1:GDN_CONV] * xpad[idx, 8:8 + t_blk, :]
        for j in range(GDN_CONV - 1):
            y = y + w[j:j + 1] * xpad[idx, 5 + j:5 + j + t_blk, :]
        xpad[idx, 0:8, :] = xpad[idx, t_blk:t_blk + 8, :]
        return y * _sigmoid(y)

    yq = conv_silu(q_ref, 0)
    yk = conv_silu(k_ref, 1)
    vs[...] = conv_silu(v_ref, 2)
    for h in range(GDN_HEADS):
        sl = slice(h * GDN_D, (h + 1) * GDN_D)
        qh = yq[:, sl]
        kh = yk[:, sl]
        qs[:, sl] = qh * (lax.rsqrt(jnp.sum(qh * qh, axis=-1, keepdims=True) + NORM_EPS) * (GDN_D ** -0.5))
        ks[:, sl] = kh * lax.rsqrt(jnp.sum(kh * kh, axis=-1, keepdims=True) + NORM_EPS)

    g = -jnp.exp(alog_ref[...]) * _softplus(a_ref[...] + dtb_ref[...])
    beta = _sigmoid(b_ref[...])
    rin = lax.broadcasted_iota(jnp.int32, (t_blk, LANES), 0) & (c_len - 1)
    s = 1
    while s < c_len:
        g = g + jnp.where(rin >= s, pltpu.roll(g, s, axis=0), 0.0)
        s *= 2
    gcol[...] = g
    bcol[...] = beta
    g_t = g.T
    b_t = beta.T
    for c in range(n_chunk):
        cols = slice(c * c_len, (c + 1) * c_len)
        grow[c] = g_t[0:8, cols]
        brow[c] = jnp.concatenate([b_t[0:8, cols], b_t[0:8, cols]], axis=1)

    ii = lax.broadcasted_iota(jnp.int32, (c_len, c_len), 0)
    jj = lax.broadcasted_iota(jnp.int32, (c_len, c_len), 1)
    incl = ii >= jj
    strict = ii > jj
    eye = jnp.where(ii == jj, 1.0, 0.0).astype(F32)
    lane_lo = lax.broadcasted_iota(jnp.int32, (c_len, 2 * c_len), 1) < c_len
    zeros_w = jnp.zeros((c_len, 2 * c_len), BF16)
    zeros_rhs = jnp.zeros((c_len, 2 * GDN_D), BF16)
    head_sl = [slice(h * GDN_D, (h + 1) * GDN_D) for h in range(GDN_HEADS)]

    chains = []
    for c in range(n_chunk):
        rows = slice(c * c_len, (c + 1) * c_len)
        g_c = gcol[rows, :]
        b_c = bcol[rows, :]
        g_r = grow[c]
        b_r = brow[c]
        for h in range(GDN_HEADS):
            chains.append(dict(
                idx=c * GDN_HEADS + h, rows=rows, sl=head_sl[h],
                q=qs[rows, head_sl[h]], k=ks[rows, head_sl[h]], v=vs[rows, head_sl[h]],
                gi=g_c[:, h:h + 1], gj=g_r[h:h + 1, :], bi=b_c[:, h:h + 1], bj2=b_r[h:h + 1, :],
                g_last=g_c[c_len - 1:c_len, h:h + 1]))
    for x in chains:
        kb = x["k"].astype(BF16)
        x["qk"] = _dot_nt(jnp.concatenate([x["q"].astype(BF16), kb], axis=0), kb)
    for x in chains:
        gam = jnp.exp(jnp.where(incl, x["gi"] - x["gj"], -jnp.inf))
        x["attn"] = (x["qk"][:c_len] * gam).astype(BF16)
        a_mat = jnp.where(strict, x["bi"] * x["qk"][c_len:] * gam, 0.0)
        x["w"] = jnp.concatenate([-a_mat, eye], axis=1)
    for _ in range(6):
        for x in chains:
            wb = x["w"].astype(BF16)
            x["r"] = _dot(wb, jnp.concatenate([wb, zeros_w], axis=0))
        for x in chains:
            x["w"] = jnp.where(lane_lo, x["r"], x["w"] + x["r"])
    for x in chains:
        eg = jnp.exp(x["gi"])
        rhs = jnp.concatenate([x["v"], x["k"] * eg], axis=1).astype(BF16)
        t_mat = (x["w"] * x["bj2"]).astype(BF16)
        x["uw"] = _dot(t_mat, jnp.concatenate([zeros_rhs, rhs], axis=0))
        x["qe"] = (x["q"] * eg).astype(BF16)
        x["kd"] = (x["k"] * jnp.exp(x["g_last"] - x["gi"])).astype(BF16)
    for x in chains:
        us[x["rows"], x["sl"]] = x["uw"][:, :GDN_D]
        wqe[x["idx"], 0:c_len, :] = x["uw"][:, GDN_D:].astype(BF16)
        wqe[x["idx"], c_len:2 * c_len, :] = x["qe"]
        attn_s[x["idx"]] = x["attn"]
        kd_s[x["idx"]] = x["kd"]

    st = [state_scr[h] for h in range(GDN_HEADS)]
    for c in range(n_chunk):
        rows = slice(c * c_len, (c + 1) * c_len)
        g_c = gcol[rows, :]
        idx = [c * GDN_HEADS + h for h in range(GDN_HEADS)]
        wq = [_dot(wqe[idx[h]], st[h].astype(BF16)) for h in range(GDN_HEADS)]
        v_new = [(us[rows, head_sl[h]] - wq[h][:c_len]).astype(BF16) for h in range(GDN_HEADS)]
        o = [wq[h][c_len:] + _dot(attn_s[idx[h]], v_new[h]) for h in range(GDN_HEADS)]
        st = [st[h] * jnp.exp(g_c[c_len - 1:c_len, h:h + 1]) + _dot_tn(kd_s[idx[h]], v_new[h])
              for h in range(GDN_HEADS)]
        for h in range(GDN_HEADS):
            us[rows, head_sl[h]] = o[h]
    for h in range(GDN_HEADS):
        state_scr[h] = st[h]

    nw = nw_ref[...]
    for h in range(GDN_HEADS):
        o = us[:, head_sl[h]]
        on = o * lax.rsqrt(jnp.mean(o * o, axis=-1, keepdims=True) + NORM_EPS) * nw
        z = z_ref[:, head_sl[h]].astype(F32)
        o_ref[:, head_sl[h]] = (on * (z * _sigmoid(z))).astype(o_ref.dtype)


def _gdn(proj, ab, cw, alog, dtb, nw, *, batch, seq, t_blk=256):
    m = batch * seq
    hw = GDN_HEADS * GDN_D
    nt = seq // t_blk
    n_chunk = t_blk // GDN_CHUNK
    tok = lambda col: pl.BlockSpec((t_blk, hw), lambda b, t, col=col: (b * nt + t, col))
    const2 = lambda shape: pl.BlockSpec(shape, lambda b, t: (0,) * len(shape))
    return pl.pallas_call(
        functools.partial(_gdn_kernel, t_blk=t_blk),
        grid=(batch, nt),
        in_specs=[
            tok(COL_GDN_Q), tok(COL_GDN_K), tok(COL_GDN_V), tok(COL_GDN_Z),
            pl.BlockSpec((t_blk, LANES), lambda b, t: (b * nt + t, 0)),
            pl.BlockSpec((t_blk, LANES), lambda b, t: (b * nt + t, 1)),
            const2((3, GDN_CONV, hw)), const2((1, LANES)), const2((1, LANES)), const2((1, GDN_D)),
        ],
        out_specs=pl.BlockSpec((t_blk, hw), lambda b, t: (b * nt + t, 0)),
        out_shape=jax.ShapeDtypeStruct((m, hw), BF16),
        scratch_shapes=[
            pltpu.VMEM((GDN_HEADS, GDN_D, GDN_D), F32),
            pltpu.VMEM((3, 8 + t_blk, hw), F32),
            pltpu.VMEM((t_blk, hw), F32), pltpu.VMEM((t_blk, hw), F32), pltpu.VMEM((t_blk, hw), F32),
            pltpu.VMEM((t_blk, LANES), F32), pltpu.VMEM((t_blk, LANES), F32),
            pltpu.VMEM((n_chunk, 8, GDN_CHUNK), F32), pltpu.VMEM((n_chunk, 8, 2 * GDN_CHUNK), F32),
            pltpu.VMEM((t_blk, hw), F32),
            pltpu.VMEM((n_chunk * GDN_HEADS, 2 * GDN_CHUNK, GDN_D), BF16),
            pltpu.VMEM((n_chunk * GDN_HEADS, GDN_CHUNK, GDN_CHUNK), BF16),
            pltpu.VMEM((n_chunk * GDN_HEADS, GDN_CHUNK, GDN_D), BF16),
        ],
        name="gdn",
        compiler_params=pltpu.CompilerParams(
            dimension_semantics=("arbitrary", "arbitrary"), vmem_limit_bytes=VMEM_LIMIT),
    )(proj, proj, proj, proj, ab, ab, cw, alog, dtb, nw)


def _diff_kernel(lam_ref, nw_ref, slope_ref, q_ref, k_ref, v_ref, z_ref, o_ref, vt_scr, acc_scr,
                 s0_scr, s1_scr, *, tq, tk, lam_init):
    qi = pl.program_id(2)
    n_sub = tq // tk
    assert n_sub * tk == tq and n_sub % LOOP_BLOCKS == 0 and LOOP_BLOCKS % 2 == 0

    @pl.when(qi == 0)
    def _():
        vt_scr[0:DIFF_DV, :] = v_ref[...].astype(F32).T.astype(BF16)
        vt_scr[DIFF_DV:, :] = jnp.ones((ONES_ROWS, vt_scr.shape[1]), BF16)

    q_t = (q_ref[...].astype(F32) * (DIFF_DH ** -0.5 * LOG2E)).T
    row = lax.broadcasted_iota(jnp.int32, (2 * DIFF_DH, tq), 0)
    qq = jnp.concatenate([jnp.where(row < DIFF_DH, q_t, 0.0), jnp.where(row >= DIFF_DH, q_t, 0.0)],
                         axis=1).astype(BF16)
    slope = slope_ref[0]
    key_pos = lax.broadcasted_iota(jnp.int32, (tk, LANES), 0).astype(F32)
    lane = lax.broadcasted_iota(jnp.int32, (tk, LANES), 1)
    b_full = slope * key_pos
    b_hi = b_full.astype(BF16).astype(F32)
    b_lo = b_full - b_hi
    k_bias = jnp.where(lane < 2, b_hi, jnp.where(lane < 4, b_lo, 0.0)).astype(BF16)
    c_full = jnp.full((LANES, 2 * tq), LOG2E, F32)
    c_hi = c_full.astype(BF16).astype(F32)
    c_lo = c_full - c_hi
    row2 = lax.broadcasted_iota(jnp.int32, (LANES, 2 * tq), 0)
    q_bias = jnp.where(row2 >= 4, 0.0, jnp.where((row2 & 1) == 0, c_hi, c_lo)).astype(BF16)
    qq = jnp.concatenate([qq, q_bias], axis=0)
    slope_l2 = slope[:, 0:1] * LOG2E
    acc_scr[...] = jnp.zeros_like(acc_scr)

    slabs = [(slice(c * tk, (c + 1) * tk), c % n_sub) for c in range(2 * n_sub)]

    def scores(kj, s_ref, sub):
        k0 = pl.multiple_of(kj * tk, tk)
        ka = jnp.concatenate([k_ref[pl.ds(k0, tk), :], k_bias], axis=1)
        if sub is None:
            s_ref[...] = _dot(ka, qq)
        else:
            for cs, pos in slabs:
                if pos >= sub:
                    s_ref[:, cs] = _dot(ka, qq[:, cs])

    def consume(kj, s_ref, m, sub):
        k0 = pl.multiple_of(kj * tk, tk)
        off = slope_l2 * (kj * tk).astype(F32)
        vt = vt_scr[:, pl.ds(k0, tk)]
        m_out = []
        for cs, pos in slabs:
            if sub is not None and pos < sub:
                m_out.append(m[:, cs])
                continue
            s = s_ref[:, cs]
            if sub is not None and pos == sub:
                q_pos = lax.broadcasted_iota(jnp.int32, (tk, tk), 1)
                key_pos_blk = lax.broadcasted_iota(jnp.int32, (tk, tk), 0)
                s = jnp.where(key_pos_blk <= q_pos, s, -jnp.inf)
            m_new = jnp.maximum(m[:, cs], jnp.max(s, axis=0, keepdims=True) + off)
            p = jnp.exp2(s - (m_new - off))
            alpha = jnp.exp2(m[:, cs] - m_new)
            m_out.append(m_new)
            acc_scr[:, cs] = acc_scr[:, cs] * alpha + _dot(vt, p.astype(BF16))
        return jnp.concatenate(m_out, axis=1)

    def group(j, m):
        for u in range(0, LOOP_BLOCKS, 2):
            kj = LOOP_BLOCKS * j + u
            scores(kj + 1, s1_scr, None)
            m = consume(kj, s0_scr, m, None)
            scores(kj + 2, s0_scr, None)
            m = consume(kj + 1, s1_scr, m, None)
        return m

    m0 = jnp.full((1, 2 * tq), -jnp.inf, F32)
    scores(0, s0_scr, None)
    n_full = qi * n_sub
    m = lax.fori_loop(0, n_full // LOOP_BLOCKS, group, m0)
    bufs = (s0_scr, s1_scr)
    for sub in range(n_sub):
        if sub + 1 < n_sub:
            scores(n_full + sub + 1, bufs[(sub + 1) % 2], sub + 1)
        m = consume(n_full + sub, bufs[sub % 2], m, sub)

    lv = lam_ref[...]
    lam = (jnp.exp(jnp.sum(lv[0:1] * lv[1:2], axis=-1, keepdims=True))
           - jnp.exp(jnp.sum(lv[2:3] * lv[3:4], axis=-1, keepdims=True)) + lam_init)
    acc = acc_scr[0:DIFF_DV, :] / acc_scr[DIFF_DV:DIFF_DV + 1, :]
    o = (acc[:, :tq] - lam * acc[:, tq:]).T
    on = o * lax.rsqrt(jnp.mean(o * o, axis=-1, keepdims=True) + NORM_EPS) * nw_ref[...] * (1.0 - lam_init)
    z = z_ref[...].astype(F32)
    o_ref[...] = (on * (z * _sigmoid(z))).astype(o_ref.dtype)


def _diff_attn(proj, lam_vecs, nw, *, batch, seq, lam_init, tq=1024, tk=256):
    m = batch * seq
    nq = seq // tq
    slopes = 2.0 ** (-8.0 * jnp.arange(1, DIFF_HEADS + 1, dtype=F32) / DIFF_HEADS)
    slopes = jnp.broadcast_to(slopes[:, None, None], (DIFF_HEADS, 1, LANES))
    return pl.pallas_call(
        functools.partial(_diff_kernel, tq=tq, tk=tk, lam_init=lam_init),
        grid=(batch, DIFF_HEADS, nq),
        in_specs=[
            pl.BlockSpec((4, DIFF_DH), lambda b, h, i: (0, 0)),
            pl.BlockSpec((1, DIFF_DV), lambda b, h, i: (0, 0)),
            pl.BlockSpec((1, 1, LANES), lambda b, h, i: (h, 0, 0)),
            pl.BlockSpec((tq, DIFF_DV), lambda b, h, i: (b * nq + i, COL_DIFF_Q + h)),
            pl.BlockSpec((seq, DIFF_DV), lambda b, h, i: (b, COL_DIFF_K + h)),
            pl.BlockSpec((seq, DIFF_DV), lambda b, h, i: (b, COL_DIFF_V + h)),
            pl.BlockSpec((tq, DIFF_DV), lambda b, h, i: (b * nq + i, COL_DIFF_Z + h)),
        ],
        out_specs=pl.BlockSpec((tq, DIFF_DV), lambda b, h, i: (b * nq + i, h)),
        out_shape=jax.ShapeDtypeStruct((m, DIFF_HEADS * DIFF_DV), BF16),
        scratch_shapes=[pltpu.VMEM((DIFF_DV + ONES_ROWS, seq), BF16),
                        pltpu.VMEM((DIFF_DV + ONES_ROWS, 2 * tq), F32),
                        pltpu.VMEM((tk, 2 * tq), F32), pltpu.VMEM((tk, 2 * tq), F32)],
        name="diffattn",
        compiler_params=pltpu.CompilerParams(
            dimension_semantics=("arbitrary", "arbitrary", "arbitrary"), vmem_limit_bytes=VMEM_LIMIT),
    )(lam_vecs, nw, slopes, proj, proj, proj, proj)


def _memkv_kernel(mem_ref, nw_ref, w_ref, o_ref):
    x = mem_ref[...]
    ms = jnp.mean(x * x, axis=-1, keepdims=True)
    hb = ((x * lax.rsqrt(ms + NORM_EPS)) * nw_ref[...]).astype(BF16)
    o_ref[...] = _dot(hb, w_ref[...]).astype(o_ref.dtype)


def _memkv(mem2d, nw, w, *, tn=1024):
    rows = mem2d.shape[0]
    width = w.shape[1]
    return pl.pallas_call(
        _memkv_kernel,
        grid=(width // tn,),
        in_specs=[
            pl.BlockSpec((rows, D_MODEL), lambda j: (0, 0)),
            pl.BlockSpec((1, D_MODEL), lambda j: (0, 0)),
            pl.BlockSpec((D_MODEL, tn), lambda j: (0, j)),
        ],
        out_specs=pl.BlockSpec((rows, tn), lambda j: (0, j)),
        out_shape=jax.ShapeDtypeStruct((rows, width), BF16),
        name="memkv",
        compiler_params=pltpu.CompilerParams(
            dimension_semantics=("arbitrary",), vmem_limit_bytes=VMEM_LIMIT),
    )(mem2d, nw, w)


def _merge_kernel(x_ref, oa_ref, ob_ref, mq_ref, mz_ref, gg_ref, gd_ref, gm_ref, mk_ref, mv_ref,
                  wg_ref, wd_ref, wm_ref, wo_ref, pw_ref, o_ref):
    heads = []
    for h in range(MEM_HEADS):
        sl = slice(h * MEM_DH, (h + 1) * MEM_DH)
        s = _dot_nt(mq_ref[:, sl], mk_ref[:, sl]) * (MEM_DH ** -0.5)
        p = jnp.exp(s - jnp.max(s, axis=-1, keepdims=True))
        p = p / jnp.sum(p, axis=-1, keepdims=True)
        heads.append(_dot(p.astype(BF16), mv_ref[:, sl]))
    mz = mz_ref[...].astype(F32)
    oc = (jnp.concatenate(heads, axis=1) * (mz * _sigmoid(mz))).astype(BF16)

    y = _sigmoid(gg_ref[...].astype(F32)) * _dot(oa_ref[...], wg_ref[...])
    y = y + _sigmoid(gd_ref[...].astype(F32)) * _dot(ob_ref[...], wd_ref[...])
    y = y + _sigmoid(gm_ref[...].astype(F32)) * _dot(oc, wm_ref[...])
    out = _dot(y.astype(BF16), wo_ref[...])
    ms = jnp.mean(out * out, axis=-1, keepdims=True)
    o_ref[...] = x_ref[...] + (out * lax.rsqrt(ms + NORM_EPS)) * pw_ref[...]


def _merge(x, proj, oa, ob, memkv, wg, wd, wm, wo, pw, *, seq, mem_len, tm=512):
    m = x.shape[0]
    per_b = seq // tm
    mem_w = MEM_HEADS * MEM_DH
    tok = lambda width, col: pl.BlockSpec((tm, width), lambda i, col=col: (i, col))
    wspec = pl.BlockSpec((D_MODEL, D_MODEL), lambda i: (0, 0))
    return pl.pallas_call(
        _merge_kernel,
        grid=(m // tm,),
        in_specs=[
            tok(D_MODEL, 0), tok(D_MODEL, 0), tok(D_MODEL, 0),
            tok(mem_w, COL_MEM_Q), tok(mem_w, COL_MEM_Z),
            tok(D_MODEL, COL_GATES), tok(D_MODEL, COL_GATES + 1), tok(D_MODEL, COL_GATES + 2),
            pl.BlockSpec((mem_len, mem_w), lambda i: (i // per_b, 0)),
            pl.BlockSpec((mem_len, mem_w), lambda i: (i // per_b, 1)),
            wspec, wspec, wspec, wspec,
            pl.BlockSpec((1, D_MODEL), lambda i: (0, 0)),
        ],
        out_specs=pl.BlockSpec((tm, D_MODEL), lambda i: (i, 0)),
        out_shape=jax.ShapeDtypeStruct((m, D_MODEL), F32),
        name="merge",
        compiler_params=pltpu.CompilerParams(
            dimension_semantics=("arbitrary",), vmem_limit_bytes=VMEM_LIMIT),
    )(x, oa, ob, proj, proj, proj, proj, proj, memkv, memkv, wg, wd, wm, wo, pw)


def kernel(x, mem, pre_norm_w, post_norm_w, w_in, gdn_conv_w, gdn_a_log, gdn_dt_bias, gdn_norm_w,
           diff_lambda, diff_norm_w, mem_norm_w, w_mem_kv, w_br_gdn, w_br_diff, w_br_mem, w_out):
    batch, seq, d = x.shape
    mem_len = mem.shape[1]
    depth = w_in.shape[0]
    assert d == D_MODEL and seq % 1024 == 0

    w_main = jnp.concatenate([w_in[:, :, :QKV_W], w_in[:, :, QKV_W + 2 * GDN_HEADS:]], axis=-1).astype(BF16)
    w_a = w_in[:, :, QKV_W:QKV_W + GDN_HEADS]
    w_b = w_in[:, :, QKV_W + GDN_HEADS:QKV_W + 2 * GDN_HEADS]
    pad = jnp.zeros((depth, D_MODEL, LANES - GDN_HEADS), w_in.dtype)
    w_ab = jnp.concatenate([w_a, pad, w_b, pad], axis=-1).astype(BF16)
    hw = GDN_HEADS * GDN_D
    conv_w = gdn_conv_w.reshape(depth, GDN_CONV, 3, hw).transpose(0, 2, 1, 3)
    lane_pad = lambda t: jnp.pad(t, ((0, 0), (0, LANES - GDN_HEADS)))[:, None, :]
    a_log = lane_pad(gdn_a_log)
    dt_bias = lane_pad(gdn_dt_bias)
    w_mem_kv_b = w_mem_kv.astype(BF16)
    w_g, w_d, w_m, w_o = (t.astype(BF16) for t in (w_br_gdn, w_br_diff, w_br_mem, w_out))

    xf = x.reshape(batch * seq, d)
    mem2d = mem.reshape(batch * mem_len, d)
    for l in range(depth):
        lam_init = 0.8 - 0.6 * math.exp(-0.3 * l)
        proj, ab = _inproj(xf, pre_norm_w[l][None], w_main[l], w_ab[l])
        oa = _gdn(proj, ab, conv_w[l], a_log[l], dt_bias[l], gdn_norm_w[l][None], batch=batch, seq=seq)
        ob = _diff_attn(proj, diff_lambda[l], diff_norm_w[l][None], batch=batch, seq=seq, lam_init=lam_init)
        memkv = _memkv(mem2d, mem_norm_w[l][None], w_mem_kv_b[l])
        xf = _merge(xf, proj, oa, ob, memkv, w_g[l], w_d[l], w_m[l], w_o[l], post_norm_w[l][None],
                    seq=seq, mem_len=mem_len)
    return xf.reshape(batch, seq, d)
```

```python
import functools
import math

import jax
import jax.numpy as jnp
from jax import lax
from jax.experimental import pallas as pl
from jax.experimental.pallas import tpu as pltpu

F32 = jnp.float32
BF16 = jnp.bfloat16

D_MODEL = 1024
NORM_EPS = 1e-6
GDN_HEADS = 8
GDN_D = 128
GDN_CONV = 4
GDN_CHUNK = 64
DIFF_HEADS = 8
DIFF_DH = 64
DIFF_DV = 2 * DIFF_DH
MEM_HEADS = 4
MEM_DH = 256
N_BRANCH = 3

LANES = 128
LOG2E = 1.4426950408889634
ONES_ROWS = 16
LOOP_BLOCKS = 4
QKV_W = 3 * GDN_HEADS * GDN_D
COL_GDN_Q, COL_GDN_K, COL_GDN_V, COL_GDN_Z = 0, 1, 2, 3
COL_DIFF_Q, COL_DIFF_K, COL_DIFF_V, COL_DIFF_Z = 32, 40, 48, 56
COL_MEM_Q, COL_MEM_Z = 8, 9
COL_GATES = 10
PROJ_W = 13 * D_MODEL
AB_W = 2 * LANES

VMEM_LIMIT = 56 * 1024 * 1024


def _sigmoid(x):
    return 1.0 / (1.0 + jnp.exp(-x))


def _softplus(x):
    return jnp.maximum(x, 0.0) + jnp.log(1.0 + jnp.exp(-jnp.abs(x)))


def _dot(a, b):
    return jnp.dot(a, b, preferred_element_type=F32)


def _dot_nt(a, b):
    return lax.dot_general(a, b, (((1,), (1,)), ((), ())), preferred_element_type=F32)


def _dot_tn(a, b):
    return lax.dot_general(a, b, (((0,), (0,)), ((), ())), preferred_element_type=F32)


def _inproj_kernel(x_ref, nw_ref, wqkv_ref, wrest_ref, wab_ref, o_ref, oab_ref, h_scr, *, n_qkv):
    j = pl.program_id(1)

    @pl.when(j == 0)
    def _():
        x = x_ref[...]
        ms = jnp.mean(x * x, axis=-1, keepdims=True)
        hb = ((x * lax.rsqrt(ms + NORM_EPS)) * nw_ref[...]).astype(BF16)
        h_scr[...] = hb
        oab_ref[...] = _dot(hb, wab_ref[...])

    @pl.when(j < n_qkv)
    def _():
        o_ref[...] = _dot(h_scr[...], wqkv_ref[...]).astype(o_ref.dtype)

    @pl.when(j >= n_qkv)
    def _():
        o_ref[...] = _dot(h_scr[...], wrest_ref[...]).astype(o_ref.dtype)


def _inproj(x, nw, w_qkv, w_rest, wab, *, tm=2048, tn=1024):
    m = x.shape[0]
    tm = min(tm, m)
    n_qkv = w_qkv.shape[1] // tn
    assert n_qkv * tn == w_qkv.shape[1] and (PROJ_W - w_qkv.shape[1]) == w_rest.shape[1]
    return pl.pallas_call(
        functools.partial(_inproj_kernel, n_qkv=n_qkv),
        grid=(m // tm, PROJ_W // tn),
        in_specs=[
            pl.BlockSpec((tm, D_MODEL), lambda i, j: (i, 0)),
            pl.BlockSpec((1, D_MODEL), lambda i, j: (0, 0)),
            pl.BlockSpec((D_MODEL, tn), lambda i, j: (0, jnp.minimum(j, n_qkv - 1))),
            pl.BlockSpec((D_MODEL, tn), lambda i, j: (0, jnp.maximum(j - n_qkv, 0))),
            pl.BlockSpec((D_MODEL, AB_W), lambda i, j: (0, 0)),
        ],
        out_specs=[
            pl.BlockSpec((tm, tn), lambda i, j: (i, j)),
            pl.BlockSpec((tm, AB_W), lambda i, j: (i, 0)),
        ],
        out_shape=[
            jax.ShapeDtypeStruct((m, PROJ_W), BF16),
            jax.ShapeDtypeStruct((m, AB_W), F32),
        ],
        scratch_shapes=[pltpu.VMEM((tm, D_MODEL), BF16)],
        name="inproj",
        compiler_params=pltpu.CompilerParams(
            dimension_semantics=("arbitrary", "arbitrary"), vmem_limit_bytes=VMEM_LIMIT),
    )(x, nw, w_qkv, w_rest, wab)


def _gdn_kernel(q_ref, k_ref, v_ref, z_ref, a_ref, b_ref, cw_ref, alog_ref, dtb_ref, nw_ref, o_ref,
                state_scr, xpad, qs, ks, vs, gcol, bcol, grow, brow, us, wqe, attn_s, kd_s, *, t_blk):
    c_len = GDN_CHUNK
    n_chunk = t_blk // c_len
    hw = GDN_HEADS * GDN_D

    @pl.when(pl.program_id(1) == 0)
    def _():
        state_scr[...] = jnp.zeros_like(state_scr)
        xpad[:, 0:8, :] = jnp.zeros((3, 8, hw), F32)

    def conv_silu(ref, idx):
        xpad[idx, 8:8 + t_blk, :] = ref[...].astype(F32)
        w = cw_ref[idx]
        y = w[GDN_CONV - 1:GDN_CONV] * xpad[idx, 8:8 + t_blk, :]
        for j in range(GDN_CONV - 1):
            y = y + w[j:j + 1] * xpad[idx, 5 + j:5 + j + t_blk, :]
        xpad[idx, 0:8, :] = xpad[idx, t_blk:t_blk + 8, :]
        return y * _sigmoid(y)

    yq = conv_silu(q_ref, 0)
    yk = conv_silu(k_ref, 1)
    vs[...] = conv_silu(v_ref, 2)
    for h in range(GDN_HEADS):
        sl = slice(h * GDN_D, (h + 1) * GDN_D)
        qh = yq[:, sl]
        kh = yk[:, sl]
        qs[:, sl] = qh * (lax.rsqrt(jnp.sum(qh * qh, axis=-1, keepdims=True) + NORM_EPS) * (GDN_D ** -0.5))
        ks[:, sl] = kh * lax.rsqrt(jnp.sum(kh * kh, axis=-1, keepdims=True) + NORM_EPS)

    g = -jnp.exp(alog_ref[...]) * _softplus(a_ref[...] + dtb_ref[...])
    beta = _sigmoid(b_ref[...])
    rin = lax.broadcasted_iota(jnp.int32, (t_blk, LANES), 0) & (c_len - 1)
    s = 1
    while s < c_len:
        g = g + jnp.where(rin >= s, pltpu.roll(g, s, axis=0), 0.0)
        s *= 2
    gcol[...] = g
    bcol[...] = beta
    g_t = g.T
    b_t = beta.T
    for c in range(n_chunk):
        cols = slice(c * c_len, (c + 1) * c_len)
        grow[c] = g_t[0:8, cols]
        brow[c] = jnp.concatenate([b_t[0:8, cols], b_t[0:8, cols]], axis=1)

    ii = lax.broadcasted_iota(jnp.int32, (c_len, c_len), 0)
    jj = lax.broadcasted_iota(jnp.int32, (c_len, c_len), 1)
    incl = ii >= jj
    strict = ii > jj
    eye = jnp.where(ii == jj, 1.0, 0.0).astype(F32)
    lane_lo = lax.broadcasted_iota(jnp.int32, (c_len, 2 * c_len), 1) < c_len
    zeros_w = jnp.zeros((c_len, 2 * c_len), BF16)
    zeros_rhs = jnp.zeros((c_len, 2 * GDN_D), BF16)
    head_sl = [slice(h * GDN_D, (h + 1) * GDN_D) for h in range(GDN_HEADS)]

    chains = []
    for c in range(n_chunk):
        rows = slice(c * c_len, (c + 1) * c_len)
        g_c = gcol[rows, :]
        b_c = bcol[rows, :]
        g_r = grow[c]
        b_r = brow[c]
        for h in range(GDN_HEADS):
            chains.append(dict(
                idx=c * GDN_HEADS + h, rows=rows, sl=head_sl[h],
                q=qs[rows, head_sl[h]], k=ks[rows, head_sl[h]], v=vs[rows, head_sl[h]],
                gi=g_c[:, h:h + 1], gj=g_r[h:h + 1, :], bi=b_c[:, h:h + 1], bj2=b_r[h:h + 1, :],
                g_last=g_c[c_len - 1:c_len, h:h + 1]))
    for x in chains:
        kb = x["k"].astype(BF16)
        x["qk"] = _dot_nt(jnp.concatenate([x["q"].astype(BF16), kb], axis=0), kb)
    for x in chains:
        gam = jnp.exp(jnp.where(incl, x["gi"] - x["gj"], -jnp.inf))
        x["attn"] = (x["qk"][:c_len] * gam).astype(BF16)
        a_mat = jnp.where(strict, x["bi"] * x["qk"][c_len:] * gam, 0.0)
        x["w"] = jnp.concatenate([-a_mat, eye], axis=1)
    for _ in range(6):
        for x in chains:
            wb = x["w"].astype(BF16)
            x["r"] = _dot(wb, jnp.concatenate([wb, zeros_w], axis=0))
        for x in chains:
            x["w"] = jnp.where(lane_lo, x["r"], x["w"] + x["r"])
    for x in chains:
        eg = jnp.exp(x["gi"])
        rhs = jnp.concatenate([x["v"], x["k"] * eg], axis=1).astype(BF16)
        t_mat = (x["w"] * x["bj2"]).astype(BF16)
        x["uw"] = _dot(t_mat, jnp.concatenate([zeros_rhs, rhs], axis=0))
        x["qe"] = (x["q"] * eg).astype(BF16)
        x["kd"] = (x["k"] * jnp.exp(x["g_last"] - x["gi"])).astype(BF16)
    for x in chains:
        us[x["rows"], x["sl"]] = x["uw"][:, :GDN_D]
        wqe[x["idx"], 0:c_len, :] = x["uw"][:, GDN_D:].astype(BF16)
        wqe[x["idx"], c_len:2 * c_len, :] = x["qe"]
        attn_s[x["idx"]] = x["attn"]
        kd_s[x["idx"]] = x["kd"]

    st = [state_scr[h] for h in range(GDN_HEADS)]
    for c in range(n_chunk):
        rows = slice(c * c_len, (c + 1) * c_len)
        g_c = gcol[rows, :]
        idx = [c * GDN_HEADS + h for h in range(GDN_HEADS)]
        wq = [_dot(wqe[idx[h]], st[h].astype(BF16)) for h in range(GDN_HEADS)]
        v_new = [(us[rows, head_sl[h]] - wq[h][:c_len]).astype(BF16) for h in range(GDN_HEADS)]
        o = [wq[h][c_len:] + _dot(attn_s[idx[h]], v_new[h]) for h in range(GDN_HEADS)]
        st = [st[h] * jnp.exp(g_c[c_len - 1:c_len, h:h + 1]) + _dot_tn(kd_s[idx[h]], v_new[h])
              for h in range(GDN_HEADS)]
        for h in range(GDN_HEADS):
            us[rows, head_sl[h]] = o[h]
    for h in range(GDN_HEADS):
        state_scr[h] = st[h]

    nw = nw_ref[...]
    for h in range(GDN_HEADS):
        o = us[:, head_sl[h]]
        on = o * lax.rsqrt(jnp.mean(o * o, axis=-1, keepdims=True) + NORM_EPS) * nw
        z = z_ref[:, head_sl[h]].astype(F32)
        o_ref[:, head_sl[h]] = (on * (z * _sigmoid(z))).astype(o_ref.dtype)


def _gdn(proj, ab, cw, alog, dtb, nw, *, batch, seq, t_blk=256):
    m = batch * seq
    hw = GDN_HEADS * GDN_D
    nt = seq // t_blk
    n_chunk = t_blk // GDN_CHUNK
    tok = lambda col: pl.BlockSpec((t_blk, hw), lambda b, t, col=col: (b * nt + t, col))
    const2 = lambda shape: pl.BlockSpec(shape, lambda b, t: (0,) * len(shape))
    return pl.pallas_call(
        functools.partial(_gdn_kernel, t_blk=t_blk),
        grid=(batch, nt),
        in_specs=[
            tok(COL_GDN_Q), tok(COL_GDN_K), tok(COL_GDN_V), tok(COL_GDN_Z),
            pl.BlockSpec((t_blk, LANES), lambda b, t: (b * nt + t, 0)),
            pl.BlockSpec((t_blk, LANES), lambda b, t: (b * nt + t, 1)),
            const2((3, GDN_CONV, hw)), const2((1, LANES)), const2((1, LANES)), const2((1, GDN_D)),
        ],
        out_specs=pl.BlockSpec((t_blk, hw), lambda b, t: (b * nt + t, 0)),
        out_shape=jax.ShapeDtypeStruct((m, hw), BF16),
        scratch_shapes=[
            pltpu.VMEM((GDN_HEADS, GDN_D, GDN_D), F32),
            pltpu.VMEM((3, 8 + t_blk, hw), F32),
            pltpu.VMEM((t_blk, hw), F32), pltpu.VMEM((t_blk, hw), F32), pltpu.VMEM((t_blk, hw), F32),
            pltpu.VMEM((t_blk, LANES), F32), pltpu.VMEM((t_blk, LANES), F32),
            pltpu.VMEM((n_chunk, 8, GDN_CHUNK), F32), pltpu.VMEM((n_chunk, 8, 2 * GDN_CHUNK), F32),
            pltpu.VMEM((t_blk, hw), F32),
            pltpu.VMEM((n_chunk * GDN_HEADS, 2 * GDN_CHUNK, GDN_D), BF16),
            pltpu.VMEM((n_chunk * GDN_HEADS, GDN_CHUNK, GDN_CHUNK), BF16),
            pltpu.VMEM((n_chunk * GDN_HEADS, GDN_CHUNK, GDN_D), BF16),
        ],
        name="gdn",
        compiler_params=pltpu.CompilerParams(
            dimension_semantics=("arbitrary", "arbitrary"), vmem_limit_bytes=VMEM_LIMIT),
    )(proj, proj, proj, proj, ab, ab, cw, alog, dtb, nw)


def _diff_kernel(lam_ref, nw_ref, slope_ref, q_ref, k_ref, v_ref, z_ref, o_ref, vt_scr, acc_scr,
                 s0_scr, s1_scr, *sd_scr, tq, tk, lam_init):
    qi = pl.program_id(2)
    n_sub = tq // tk
    assert n_sub * tk == tq and n_sub % LOOP_BLOCKS == 0 and LOOP_BLOCKS % 2 == 0

    @pl.when(qi == 0)
    def _():
        vt_scr[0:DIFF_DV, :] = v_ref[...].astype(F32).T.astype(BF16)
        vt_scr[DIFF_DV:, :] = jnp.ones((ONES_ROWS, vt_scr.shape[1]), BF16)

    q_t = (q_ref[...].astype(F32) * (DIFF_DH ** -0.5 * LOG2E)).T
    row = lax.broadcasted_iota(jnp.int32, (2 * DIFF_DH, tq), 0)
    qq = jnp.concatenate([jnp.where(row < DIFF_DH, q_t, 0.0), jnp.where(row >= DIFF_DH, q_t, 0.0)],
                         axis=1).astype(BF16)
    slope = slope_ref[0]
    key_pos = lax.broadcasted_iota(jnp.int32, (tk, LANES), 0).astype(F32)
    lane = lax.broadcasted_iota(jnp.int32, (tk, LANES), 1)
    b_full = slope * key_pos
    b_hi = b_full.astype(BF16).astype(F32)
    b_lo = b_full - b_hi
    k_bias = jnp.where(lane < 2, b_hi, jnp.where(lane < 4, b_lo, 0.0)).astype(BF16)
    c_full = jnp.full((LANES, 2 * tq), LOG2E, F32)
    c_hi = c_full.astype(BF16).astype(F32)
    c_lo = c_full - c_hi
    row2 = lax.broadcasted_iota(jnp.int32, (LANES, 2 * tq), 0)
    q_bias = jnp.where(row2 >= 4, 0.0, jnp.where((row2 & 1) == 0, c_hi, c_lo)).astype(BF16)
    qq = jnp.concatenate([qq, q_bias], axis=0)
    slope_l2 = slope[:, 0:1] * LOG2E
    acc_scr[...] = jnp.zeros_like(acc_scr)

    slabs = [(slice(c * tk, (c + 1) * tk), c % n_sub) for c in range(2 * n_sub)]

    def scores(kj, s_ref, sub):
        k0 = pl.multiple_of(kj * tk, tk)
        ka = jnp.concatenate([k_ref[pl.ds(k0, tk), :], k_bias], axis=1)
        if sub is None:
            s_ref[...] = _dot(ka, qq)
        else:
            for cs, pos in slabs:
                if pos >= sub:
                    s_ref[:, cs] = _dot(ka, qq[:, cs])

    def consume(kj, s_ref, m):
        k0 = pl.multiple_of(kj * tk, tk)
        off = slope_l2 * (kj * tk).astype(F32)
        vt = vt_scr[:, pl.ds(k0, tk)]
        m_out = []
        for cs, _ in slabs:
            s = s_ref[:, cs]
            m_new = jnp.maximum(m[:, cs], jnp.max(s, axis=0, keepdims=True) + off)
            p = jnp.exp2(s - (m_new - off))
            alpha = jnp.exp2(m[:, cs] - m_new)
            m_out.append(m_new)
            acc_scr[:, cs] = acc_scr[:, cs] * alpha + _dot(vt, p.astype(BF16))
        return jnp.concatenate(m_out, axis=1)

    def group(j, m):
        for u in range(0, LOOP_BLOCKS, 2):
            kj = LOOP_BLOCKS * j + u
            scores(kj + 1, s1_scr, None)
            m = consume(kj, s0_scr, m)
            scores(kj + 2, s0_scr, None)
            m = consume(kj + 1, s1_scr, m)
        return m

    m0 = jnp.full((1, 2 * tq), -jnp.inf, F32)
    scores(0, s0_scr, None)
    n_full = qi * n_sub
    m = lax.fori_loop(0, n_full // LOOP_BLOCKS, group, m0)

    dbufs = (s0_scr, s1_scr) + tuple(sd_scr)
    for sub in range(1, n_sub):
        scores(n_full + sub, dbufs[sub], sub)
    causal = lax.broadcasted_iota(jnp.int32, (tk, tk), 0) <= lax.broadcasted_iota(jnp.int32, (tk, tk), 1)
    offs = [slope_l2 * ((n_full + sub) * tk).astype(F32) for sub in range(n_sub)]
    for cs, pos in slabs:
        s_blk = [dbufs[sub][:, cs] for sub in range(pos)] + [jnp.where(causal, dbufs[pos][:, cs], -jnp.inf)]
        m_old = m[:, cs]
        m_new = m_old
        for sub in range(pos + 1):
            m_new = jnp.maximum(m_new, jnp.max(s_blk[sub], axis=0, keepdims=True) + offs[sub])
        pv = None
        for sub in range(pos + 1):
            p = jnp.exp2(s_blk[sub] - (m_new - offs[sub])).astype(BF16)
            k0 = pl.multiple_of((n_full + sub) * tk, tk)
            d = _dot(vt_scr[:, pl.ds(k0, tk)], p)
            pv = d if pv is None else pv + d
        acc_scr[:, cs] = acc_scr[:, cs] * jnp.exp2(m_old - m_new) + pv

    lv = lam_ref[...]
    lam = (jnp.exp(jnp.sum(lv[0:1] * lv[1:2], axis=-1, keepdims=True))
           - jnp.exp(jnp.sum(lv[2:3] * lv[3:4], axis=-1, keepdims=True)) + lam_init)
    inv_l = 1.0 / acc_scr[DIFF_DV:DIFF_DV + 1, :]
    o = (acc_scr[0:DIFF_DV, 0:tq] * inv_l[:, :tq] - acc_scr[0:DIFF_DV, tq:] * (lam * inv_l[:, tq:])).T
    on = o * lax.rsqrt(jnp.mean(o * o, axis=-1, keepdims=True) + NORM_EPS) * nw_ref[...] * (1.0 - lam_init)
    z = z_ref[...].astype(F32)
    o_ref[...] = (on * (z * _sigmoid(z))).astype(o_ref.dtype)


def _diff_attn(proj, lam_vecs, nw, *, batch, seq, lam_init, tq=1024, tk=256):
    m = batch * seq
    nq = seq // tq
    slopes = 2.0 ** (-8.0 * jnp.arange(1, DIFF_HEADS + 1, dtype=F32) / DIFF_HEADS)
    slopes = jnp.broadcast_to(slopes[:, None, None], (DIFF_HEADS, 1, LANES))
    return pl.pallas_call(
        functools.partial(_diff_kernel, tq=tq, tk=tk, lam_init=lam_init),
        grid=(batch, DIFF_HEADS, nq),
        in_specs=[
            pl.BlockSpec((4, DIFF_DH), lambda b, h, i: (0, 0)),
            pl.BlockSpec((1, DIFF_DV), lambda b, h, i: (0, 0)),
            pl.BlockSpec((1, 1, LANES), lambda b, h, i: (h, 0, 0)),
            pl.BlockSpec((tq, DIFF_DV), lambda b, h, i: (b * nq + i, COL_DIFF_Q + h)),
            pl.BlockSpec((seq, DIFF_DV), lambda b, h, i: (b, COL_DIFF_K + h)),
            pl.BlockSpec((seq, DIFF_DV), lambda b, h, i: (b, COL_DIFF_V + h)),
            pl.BlockSpec((tq, DIFF_DV), lambda b, h, i: (b * nq + i, COL_DIFF_Z + h)),
        ],
        out_specs=pl.BlockSpec((tq, DIFF_DV), lambda b, h, i: (b * nq + i, h)),
        out_shape=jax.ShapeDtypeStruct((m, DIFF_HEADS * DIFF_DV), BF16),
        scratch_shapes=[pltpu.VMEM((DIFF_DV + ONES_ROWS, seq), BF16),
                        pltpu.VMEM((DIFF_DV + ONES_ROWS, 2 * tq), F32),
                        ] + [pltpu.VMEM((tk, 2 * tq), F32)] * (tq // tk),
        name="diffattn",
        compiler_params=pltpu.CompilerParams(
            dimension_semantics=("arbitrary", "arbitrary", "arbitrary"), vmem_limit_bytes=VMEM_LIMIT),
    )(lam_vecs, nw, slopes, proj, proj, proj, proj)


def _memkv_kernel(mem_ref, nw_ref, w_ref, o_ref):
    x = mem_ref[...]
    ms = jnp.mean(x * x, axis=-1, keepdims=True)
    hb = ((x * lax.rsqrt(ms + NORM_EPS)) * nw_ref[...]).astype(BF16)
    o_ref[...] = _dot(hb, w_ref[...]).astype(o_ref.dtype)


def _memkv(mem2d, nw, w, *, tn=1024):
    rows = mem2d.shape[0]
    width = w.shape[1]
    return pl.pallas_call(
        _memkv_kernel,
        grid=(width // tn,),
        in_specs=[
            pl.BlockSpec((rows, D_MODEL), lambda j: (0, 0)),
            pl.BlockSpec((1, D_MODEL), lambda j: (0, 0)),
            pl.BlockSpec((D_MODEL, tn), lambda j: (0, j)),
        ],
        out_specs=pl.BlockSpec((rows, tn), lambda j: (0, j)),
        out_shape=jax.ShapeDtypeStruct((rows, width), BF16),
        name="memkv",
        compiler_params=pltpu.CompilerParams(
            dimension_semantics=("arbitrary",), vmem_limit_bytes=VMEM_LIMIT),
    )(mem2d, nw, w)


def _merge_kernel(x_ref, oa_ref, ob_ref, mq_ref, mz_ref, gg_ref, gd_ref, gm_ref, mk_ref, mv_ref,
                  wg_ref, wd_ref, wm_ref, wo_ref, pw_ref, o_ref):
    heads = []
    for h in range(MEM_HEADS):
        sl = slice(h * MEM_DH, (h + 1) * MEM_DH)
        s = _dot_nt(mq_ref[:, sl], mk_ref[:, sl]) * (MEM_DH ** -0.5)
        p = jnp.exp(s - jnp.max(s, axis=-1, keepdims=True))
        p = p / jnp.sum(p, axis=-1, keepdims=True)
        heads.append(_dot(p.astype(BF16), mv_ref[:, sl]))
    mz = mz_ref[...].astype(F32)
    oc = (jnp.concatenate(heads, axis=1) * (mz * _sigmoid(mz))).astype(BF16)

    y = _sigmoid(gg_ref[...].astype(F32)) * _dot(oa_ref[...], wg_ref[...])
    y = y + _sigmoid(gd_ref[...].astype(F32)) * _dot(ob_ref[...], wd_ref[...])
    y = y + _sigmoid(gm_ref[...].astype(F32)) * _dot(oc, wm_ref[...])
    out = _dot(y.astype(BF16), wo_ref[...])
    ms = jnp.mean(out * out, axis=-1, keepdims=True)
    o_ref[...] = x_ref[...] + (out * lax.rsqrt(ms + NORM_EPS)) * pw_ref[...]


def _merge(x, proj, oa, ob, memkv, wg, wd, wm, wo, pw, *, seq, mem_len, tm=512):
    m = x.shape[0]
    per_b = seq // tm
    mem_w = MEM_HEADS * MEM_DH
    tok = lambda width, col: pl.BlockSpec((tm, width), lambda i, col=col: (i, col))
    wspec = pl.BlockSpec((D_MODEL, D_MODEL), lambda i: (0, 0))
    return pl.pallas_call(
        _merge_kernel,
        grid=(m // tm,),
        in_specs=[
            tok(D_MODEL, 0), tok(D_MODEL, 0), tok(D_MODEL, 0),
            tok(mem_w, COL_MEM_Q), tok(mem_w, COL_MEM_Z),
            tok(D_MODEL, COL_GATES), tok(D_MODEL, COL_GATES + 1), tok(D_MODEL, COL_GATES + 2),
            pl.BlockSpec((mem_len, mem_w), lambda i: (i // per_b, 0)),
            pl.BlockSpec((mem_len, mem_w), lambda i: (i // per_b, 1)),
            wspec, wspec, wspec, wspec,
            pl.BlockSpec((1, D_MODEL), lambda i: (0, 0)),
        ],
        out_specs=pl.BlockSpec((tm, D_MODEL), lambda i: (i, 0)),
        out_shape=jax.ShapeDtypeStruct((m, D_MODEL), F32),
        name="merge",
        compiler_params=pltpu.CompilerParams(
            dimension_semantics=("arbitrary",), vmem_limit_bytes=VMEM_LIMIT),
    )(x, oa, ob, proj, proj, proj, proj, proj, memkv, memkv, wg, wd, wm, wo, pw)


def kernel(x, mem, pre_norm_w, post_norm_w, w_in, gdn_conv_w, gdn_a_log, gdn_dt_bias, gdn_norm_w,
           diff_lambda, diff_norm_w, mem_norm_w, w_mem_kv, w_br_gdn, w_br_diff, w_br_mem, w_out):
    batch, seq, d = x.shape
    mem_len = mem.shape[1]
    depth = w_in.shape[0]
    assert d == D_MODEL and seq % 1024 == 0

    w_qkv = w_in[:, :, :QKV_W].astype(BF16)
    w_rest = w_in[:, :, QKV_W + 2 * GDN_HEADS:].astype(BF16)
    w_a = w_in[:, :, QKV_W:QKV_W + GDN_HEADS]
    w_b = w_in[:, :, QKV_W + GDN_HEADS:QKV_W + 2 * GDN_HEADS]
    pad = jnp.zeros((depth, D_MODEL, LANES - GDN_HEADS), w_in.dtype)
    w_ab = jnp.concatenate([w_a, pad, w_b, pad], axis=-1).astype(BF16)
    hw = GDN_HEADS * GDN_D
    conv_w = gdn_conv_w.reshape(depth, GDN_CONV, 3, hw).transpose(0, 2, 1, 3)
    lane_pad = lambda t: jnp.pad(t, ((0, 0), (0, LANES - GDN_HEADS)))[:, None, :]
    a_log = lane_pad(gdn_a_log)
    dt_bias = lane_pad(gdn_dt_bias)
    w_mem_kv_b = w_mem_kv.astype(BF16)
    w_g, w_d, w_m, w_o = (t.astype(BF16) for t in (w_br_gdn, w_br_diff, w_br_mem, w_out))

    xf = x.reshape(batch * seq, d)
    mem2d = mem.reshape(batch * mem_len, d)
    for l in range(depth):
        lam_init = 0.8 - 0.6 * math.exp(-0.3 * l)
        proj, ab = _inproj(xf, pre_norm_w[l][None], w_qkv[l], w_rest[l], w_ab[l])
        oa = _gdn(proj, ab, conv_w[l], a_log[l], dt_bias[l], gdn_norm_w[l][None], batch=batch, seq=seq)
        ob = _diff_attn(proj, diff_lambda[l], diff_norm_w[l][None], batch=batch, seq=seq, lam_init=lam_init)
        memkv = _memkv(mem2d, mem_norm_w[l][None], w_mem_kv_b[l])
        xf = _merge(xf, proj, oa, ob, memkv, w_g[l], w_d[l], w_m[l], w_o[l], post_norm_w[l][None],
                    seq=seq, mem_len=mem_len)
    return xf.reshape(batch, seq, d)
```

```python
import functools
import math

import jax
import jax.numpy as jnp
from jax import lax
from jax.experimental import pallas as pl
from jax.experimental.pallas import tpu as pltpu

F32 = jnp.float32
BF16 = jnp.bfloat16

D_MODEL = 1024
NORM_EPS = 1e-6
GDN_HEADS = 8
GDN_D = 128
GDN_CONV = 4
GDN_CHUNK = 64
DIFF_HEADS = 8
DIFF_DH = 64
DIFF_DV = 2 * DIFF_DH
MEM_HEADS = 4
MEM_DH = 256
N_BRANCH = 3

LANES = 128
LOG2E = 1.4426950408889634
ONES_ROWS = 16
LOOP_BLOCKS = 4
QKV_W = 3 * GDN_HEADS * GDN_D
COL_GDN_Q, COL_GDN_K, COL_GDN_V, COL_GDN_Z = 0, 1, 2, 3
COL_DIFF_Q, COL_DIFF_K, COL_DIFF_V, COL_DIFF_Z = 32, 40, 48, 56
COL_MEM_Q, COL_MEM_Z = 8, 9
COL_GATES = 10
PROJ_W = 13 * D_MODEL
AB_W = 2 * LANES
AB_SKIP = 2 * GDN_HEADS

VMEM_LIMIT = 56 * 1024 * 1024


def _sigmoid(x):
    return 1.0 / (1.0 + jnp.exp(-x))


def _softplus(x):
    return jnp.maximum(x, 0.0) + jnp.log(1.0 + jnp.exp(-jnp.abs(x)))


def _dot(a, b):
    return jnp.dot(a, b, preferred_element_type=F32)


def _dot_nt(a, b):
    return lax.dot_general(a, b, (((1,), (1,)), ((), ())), preferred_element_type=F32)


def _dot_tn(a, b):
    return lax.dot_general(a, b, (((0,), (0,)), ((), ())), preferred_element_type=F32)


def _inproj_kernel(x_ref, nw_ref, w_ref, wnext_ref, wab_ref, o_ref, oab_ref, h_scr, *, n_qkv):
    j = pl.program_id(1)
    tn = w_ref.shape[1]

    @pl.when(j == 0)
    def _():
        x = x_ref[...]
        ms = jnp.mean(x * x, axis=-1, keepdims=True)
        hb = ((x * lax.rsqrt(ms + NORM_EPS)) * nw_ref[...]).astype(BF16)
        h_scr[...] = hb
        oab_ref[...] = _dot(hb, wab_ref[...])

    @pl.when(j < n_qkv)
    def _():
        o_ref[...] = _dot(h_scr[...], w_ref[...].astype(BF16)).astype(o_ref.dtype)

    @pl.when(j >= n_qkv)
    def _():
        w = jnp.concatenate([w_ref[:, AB_SKIP:], wnext_ref[:, :AB_SKIP]], axis=1).astype(BF16)
        o_ref[...] = _dot(h_scr[...], w).astype(o_ref.dtype)


def _inproj(x, nw, w_in, layer, wab, *, tm=2048, tn=1024):
    m = x.shape[0]
    tm = min(tm, m)
    n_qkv = QKV_W // tn
    assert n_qkv * tn == QKV_W and w_in.shape[2] == PROJ_W + AB_SKIP
    return pl.pallas_call(
        functools.partial(_inproj_kernel, n_qkv=n_qkv),
        grid=(m // tm, PROJ_W // tn),
        in_specs=[
            pl.BlockSpec((tm, D_MODEL), lambda i, j: (i, 0)),
            pl.BlockSpec((1, D_MODEL), lambda i, j: (0, 0)),
            pl.BlockSpec((None, D_MODEL, tn), lambda i, j: (layer, 0, j)),
            pl.BlockSpec((None, D_MODEL, LANES), lambda i, j: (layer, 0, (j + 1) * (tn // LANES))),
            pl.BlockSpec((D_MODEL, AB_W), lambda i, j: (0, 0)),
        ],
        out_specs=[
            pl.BlockSpec((tm, tn), lambda i, j: (i, j)),
            pl.BlockSpec((tm, AB_W), lambda i, j: (i, 0)),
        ],
        out_shape=[
            jax.ShapeDtypeStruct((m, PROJ_W), BF16),
            jax.ShapeDtypeStruct((m, AB_W), F32),
        ],
        scratch_shapes=[pltpu.VMEM((tm, D_MODEL), BF16)],
        name="inproj",
        compiler_params=pltpu.CompilerParams(
            dimension_semantics=("arbitrary", "arbitrary"), vmem_limit_bytes=VMEM_LIMIT),
    )(x, nw, w_in, w_in, wab)


def _gdn_kernel(q_ref, k_ref, v_ref, z_ref, a_ref, b_ref, cw_ref, alog_ref, dtb_ref, nw_ref, o_ref,
                state_scr, xpad, qs, ks, vs, gcol, bcol, grow, brow, us, wqe, attn_s, kd_s, *, t_blk):
    c_len = GDN_CHUNK
    n_chunk = t_blk // c_len
    hw = GDN_HEADS * GDN_D

    @pl.when(pl.program_id(1) == 0)
    def _():
        state_scr[...] = jnp.zeros_like(state_scr)
        xpad[:, 0:8, :] = jnp.zeros((3, 8, hw), F32)

    def conv_silu(ref, idx):
        xpad[idx, 8:8 + t_blk, :] = ref[...].astype(F32)
        w = cw_ref[idx]
        y = w[GDN_CONV - 1:GDN_CONV] * xpad[idx, 8:8 + t_blk, :]
        for j in range(GDN_CONV - 1):
            y = y + w[j:j + 1] * xpad[idx, 5 + j:5 + j + t_blk, :]
        xpad[idx, 0:8, :] = xpad[idx, t_blk:t_blk + 8, :]
        return y * _sigmoid(y)

    yq = conv_silu(q_ref, 0)
    yk = conv_silu(k_ref, 1)
    vs[...] = conv_silu(v_ref, 2)
    for h in range(GDN_HEADS):
        sl = slice(h * GDN_D, (h + 1) * GDN_D)
        qh = yq[:, sl]
        kh = yk[:, sl]
        qs[:, sl] = qh * (lax.rsqrt(jnp.sum(qh * qh, axis=-1, keepdims=True) + NORM_EPS) * (GDN_D ** -0.5))
        ks[:, sl] = kh * lax.rsqrt(jnp.sum(kh * kh, axis=-1, keepdims=True) + NORM_EPS)

    g = -jnp.exp(alog_ref[...]) * _softplus(a_ref[...] + dtb_ref[...])
    beta = _sigmoid(b_ref[...])
    rin = lax.broadcasted_iota(jnp.int32, (t_blk, LANES), 0) & (c_len - 1)
    s = 1
    while s < c_len:
        g = g + jnp.where(rin >= s, pltpu.roll(g, s, axis=0), 0.0)
        s *= 2
    gcol[...] = g
    bcol[...] = beta
    g_t = g.T
    b_t = beta.T
    for c in range(n_chunk):
        cols = slice(c * c_len, (c + 1) * c_len)
        grow[c] = g_t[0:8, cols]
        brow[c] = jnp.concatenate([b_t[0:8, cols], b_t[0:8, cols]], axis=1)

    ii = lax.broadcasted_iota(jnp.int32, (c_len, c_len), 0)
    jj = lax.broadcasted_iota(jnp.int32, (c_len, c_len), 1)
    incl = ii >= jj
    strict = ii > jj
    eye = jnp.where(ii == jj, 1.0, 0.0).astype(F32)
    lane_lo = lax.broadcasted_iota(jnp.int32, (c_len, 2 * c_len), 1) < c_len
    zeros_w = jnp.zeros((c_len, 2 * c_len), BF16)
    zeros_rhs = jnp.zeros((c_len, 2 * GDN_D), BF16)
    head_sl = [slice(h * GDN_D, (h + 1) * GDN_D) for h in range(GDN_HEADS)]

    chains = []
    for c in range(n_chunk):
        rows = slice(c * c_len, (c + 1) * c_len)
        g_c = gcol[rows, :]
        b_c = bcol[rows, :]
        g_r = grow[c]
        b_r = brow[c]
        for h in range(GDN_HEADS):
            chains.append(dict(
                idx=c * GDN_HEADS + h, rows=rows, sl=head_sl[h],
                q=qs[rows, head_sl[h]], k=ks[rows, head_sl[h]], v=vs[rows, head_sl[h]],
                gi=g_c[:, h:h + 1], gj=g_r[h:h + 1, :], bi=b_c[:, h:h + 1], bj2=b_r[h:h + 1, :],
                g_last=g_c[c_len - 1:c_len, h:h + 1]))
    for x in chains:
        kb = x["k"].astype(BF16)
        x["qk"] = _dot_nt(jnp.concatenate([x["q"].astype(BF16), kb], axis=0), kb)
    for x in chains:
        gam = jnp.exp(jnp.where(incl, x["gi"] - x["gj"], -jnp.inf))
        x["attn"] = (x["qk"][:c_len] * gam).astype(BF16)
        a_mat = jnp.where(strict, x["bi"] * x["qk"][c_len:] * gam, 0.0)
        x["w"] = jnp.concatenate([-a_mat, eye], axis=1)
    for _ in range(6):
        for x in chains:
            wb = x["w"].astype(BF16)
            x["r"] = _dot(wb, jnp.concatenate([wb, zeros_w], axis=0))
        for x in chains:
            x["w"] = jnp.where(lane_lo, x["r"], x["w"] + x["r"])
    for x in chains:
        eg = jnp.exp(x["gi"])
        rhs = jnp.concatenate([x["v"], x["k"] * eg], axis=1).astype(BF16)
        t_mat = (x["w"] * x["bj2"]).astype(BF16)
        x["uw"] = _dot(t_mat, jnp.concatenate([zeros_rhs, rhs], axis=0))
        x["qe"] = (x["q"] * eg).astype(BF16)
        x["kd"] = (x["k"] * jnp.exp(x["g_last"] - x["gi"])).astype(BF16)
    for x in chains:
        us[x["rows"], x["sl"]] = x["uw"][:, :GDN_D]
        wqe[x["idx"], 0:c_len, :] = x["uw"][:, GDN_D:].astype(BF16)
        wqe[x["idx"], c_len:2 * c_len, :] = x["qe"]
        attn_s[x["idx"]] = x["attn"]
        kd_s[x["idx"]] = x["kd"]

    st = [state_scr[h] for h in range(GDN_HEADS)]
    for c in range(n_chunk):
        rows = slice(c * c_len, (c + 1) * c_len)
        g_c = gcol[rows, :]
        idx = [c * GDN_HEADS + h for h in range(GDN_HEADS)]
        wq = [_dot(wqe[idx[h]], st[h].astype(BF16)) for h in range(GDN_HEADS)]
        v_new = [(us[rows, head_sl[h]] - wq[h][:c_len]).astype(BF16) for h in range(GDN_HEADS)]
        o = [wq[h][c_len:] + _dot(attn_s[idx[h]], v_new[h]) for h in range(GDN_HEADS)]
        st = [st[h] * jnp.exp(g_c[c_len - 1:c_len, h:h + 1]) + _dot_tn(kd_s[idx[h]], v_new[h])
              for h in range(GDN_HEADS)]
        for h in range(GDN_HEADS):
            us[rows, head_sl[h]] = o[h]
    for h in range(GDN_HEADS):
        state_scr[h] = st[h]

    nw = nw_ref[...]
    for h in range(GDN_HEADS):
        o = us[:, head_sl[h]]
        on = o * lax.rsqrt(jnp.mean(o * o, axis=-1, keepdims=True) + NORM_EPS) * nw
        z = z_ref[:, head_sl[h]].astype(F32)
        o_ref[:, head_sl[h]] = (on * (z * _sigmoid(z))).astype(o_ref.dtype)


def _gdn(proj, ab, cw, alog, dtb, nw, *, batch, seq, t_blk=256):
    m = batch * seq
    hw = GDN_HEADS * GDN_D
    nt = seq // t_blk
    n_chunk = t_blk // GDN_CHUNK
    tok = lambda col: pl.BlockSpec((t_blk, hw), lambda b, t, col=col: (b * nt + t, col))
    const2 = lambda shape: pl.BlockSpec(shape, lambda b, t: (0,) * len(shape))
    return pl.pallas_call(
        functools.partial(_gdn_kernel, t_blk=t_blk),
        grid=(batch, nt),
        in_specs=[
            tok(COL_GDN_Q), tok(COL_GDN_K), tok(COL_GDN_V), tok(COL_GDN_Z),
            pl.BlockSpec((t_blk, LANES), lambda b, t: (b * nt + t, 0)),
            pl.BlockSpec((t_blk, LANES), lambda b, t: (b * nt + t, 1)),
            const2((3, GDN_CONV, hw)), const2((1, LANES)), const2((1, LANES)), const2((1, GDN_D)),
        ],
        out_specs=pl.BlockSpec((t_blk, hw), lambda b, t: (b * nt + t, 0)),
        out_shape=jax.ShapeDtypeStruct((m, hw), BF16),
        scratch_shapes=[
            pltpu.VMEM((GDN_HEADS, GDN_D, GDN_D), F32),
            pltpu.VMEM((3, 8 + t_blk, hw), F32),
            pltpu.VMEM((t_blk, hw), F32), pltpu.VMEM((t_blk, hw), F32), pltpu.VMEM((t_blk, hw), F32),
            pltpu.VMEM((t_blk, LANES), F32), pltpu.VMEM((t_blk, LANES), F32),
            pltpu.VMEM((n_chunk, 8, GDN_CHUNK), F32), pltpu.VMEM((n_chunk, 8, 2 * GDN_CHUNK), F32),
            pltpu.VMEM((t_blk, hw), F32),
            pltpu.VMEM((n_chunk * GDN_HEADS, 2 * GDN_CHUNK, GDN_D), BF16),
            pltpu.VMEM((n_chunk * GDN_HEADS, GDN_CHUNK, GDN_CHUNK), BF16),
            pltpu.VMEM((n_chunk * GDN_HEADS, GDN_CHUNK, GDN_D), BF16),
        ],
        name="gdn",
        compiler_params=pltpu.CompilerParams(
            dimension_semantics=("arbitrary", "arbitrary"), vmem_limit_bytes=VMEM_LIMIT),
    )(proj, proj, proj, proj, ab, ab, cw, alog, dtb, nw)


def _diff_kernel(lam_ref, nw_ref, slope_ref, q_ref, k_ref, v_ref, z_ref, o_ref, vt_scr, acc_scr,
                 s0_scr, s1_scr, *sd_scr, tq, tk, lam_init):
    qi = pl.program_id(2)
    n_sub = tq // tk
    assert n_sub * tk == tq and n_sub % LOOP_BLOCKS == 0 and LOOP_BLOCKS % 2 == 0

    @pl.when(qi == 0)
    def _():
        vt_scr[0:DIFF_DV, :] = v_ref[...].astype(F32).T.astype(BF16)
        vt_scr[DIFF_DV:, :] = jnp.ones((ONES_ROWS, vt_scr.shape[1]), BF16)

    q_t = (q_ref[...].astype(F32) * (DIFF_DH ** -0.5 * LOG2E)).T
    row = lax.broadcasted_iota(jnp.int32, (2 * DIFF_DH, tq), 0)
    qq = jnp.concatenate([jnp.where(row < DIFF_DH, q_t, 0.0), jnp.where(row >= DIFF_DH, q_t, 0.0)],
                         axis=1).astype(BF16)
    slope = slope_ref[0]
    key_pos = lax.broadcasted_iota(jnp.int32, (tk, LANES), 0).astype(F32)
    lane = lax.broadcasted_iota(jnp.int32, (tk, LANES), 1)
    b_full = slope * key_pos
    b_hi = b_full.astype(BF16).astype(F32)
    b_lo = b_full - b_hi
    k_bias = jnp.where(lane < 2, b_hi, jnp.where(lane < 4, b_lo, 0.0)).astype(BF16)
    c_full = jnp.full((LANES, 2 * tq), LOG2E, F32)
    c_hi = c_full.astype(BF16).astype(F32)
    c_lo = c_full - c_hi
    row2 = lax.broadcasted_iota(jnp.int32, (LANES, 2 * tq), 0)
    q_bias = jnp.where(row2 >= 4, 0.0, jnp.where((row2 & 1) == 0, c_hi, c_lo)).astype(BF16)
    qq = jnp.concatenate([qq, q_bias], axis=0)
    slope_l2 = slope[:, 0:1] * LOG2E
    acc_scr[...] = jnp.zeros_like(acc_scr)

    slabs = [(slice(c * tk, (c + 1) * tk), c % n_sub) for c in range(2 * n_sub)]

    def scores(kj, s_ref, sub):
        k0 = pl.multiple_of(kj * tk, tk)
        ka = jnp.concatenate([k_ref[pl.ds(k0, tk), :], k_bias], axis=1)
        if sub is None:
            s_ref[...] = _dot(ka, qq)
        else:
            for cs, pos in slabs:
                if pos >= sub:
                    s_ref[:, cs] = _dot(ka, qq[:, cs])

    def consume(kj, s_ref, m):
        k0 = pl.multiple_of(kj * tk, tk)
        off = slope_l2 * (kj * tk).astype(F32)
        vt = vt_scr[:, pl.ds(k0, tk)]
        m_out = []
        for cs, _ in slabs:
            s = s_ref[:, cs]
            m_new = jnp.maximum(m[:, cs], jnp.max(s, axis=0, keepdims=True) + off)
            p = jnp.exp2(s - (m_new - off))
            alpha = jnp.exp2(m[:, cs] - m_new)
            m_out.append(m_new)
            acc_scr[:, cs] = acc_scr[:, cs] * alpha + _dot(vt, p.astype(BF16))
        return jnp.concatenate(m_out, axis=1)

    def group(j, m):
        for u in range(0, LOOP_BLOCKS, 2):
            kj = LOOP_BLOCKS * j + u
            scores(kj + 1, s1_scr, None)
            m = consume(kj, s0_scr, m)
            scores(kj + 2, s0_scr, None)
            m = consume(kj + 1, s1_scr, m)
        return m

    m0 = jnp.full((1, 2 * tq), -jnp.inf, F32)
    scores(0, s0_scr, None)
    n_full = qi * n_sub
    m = lax.fori_loop(0, n_full // LOOP_BLOCKS, group, m0)

    dbufs = (s0_scr, s1_scr) + tuple(sd_scr)
    for sub in range(1, n_sub):
        scores(n_full + sub, dbufs[sub], sub)
    causal = lax.broadcasted_iota(jnp.int32, (tk, tk), 0) <= lax.broadcasted_iota(jnp.int32, (tk, tk), 1)
    offs = [slope_l2 * ((n_full + sub) * tk).astype(F32) for sub in range(n_sub)]
    for cs, pos in slabs:
        s_blk = [dbufs[sub][:, cs] for sub in range(pos)] + [jnp.where(causal, dbufs[pos][:, cs], -jnp.inf)]
        m_old = m[:, cs]
        m_new = m_old
        for sub in range(pos + 1):
            m_new = jnp.maximum(m_new, jnp.max(s_blk[sub], axis=0, keepdims=True) + offs[sub])
        pv = None
        for sub in range(pos + 1):
            p = jnp.exp2(s_blk[sub] - (m_new - offs[sub])).astype(BF16)
            k0 = pl.multiple_of((n_full + sub) * tk, tk)
            d = _dot(vt_scr[:, pl.ds(k0, tk)], p)
            pv = d if pv is None else pv + d
        acc_scr[:, cs] = acc_scr[:, cs] * jnp.exp2(m_old - m_new) + pv

    lv = lam_ref[...]
    lam = (jnp.exp(jnp.sum(lv[0:1] * lv[1:2], axis=-1, keepdims=True))
           - jnp.exp(jnp.sum(lv[2:3] * lv[3:4], axis=-1, keepdims=True)) + lam_init)
    inv_l = 1.0 / acc_scr[DIFF_DV:DIFF_DV + 1, :]
    o = (acc_scr[0:DIFF_DV, 0:tq] * inv_l[:, :tq] - acc_scr[0:DIFF_DV, tq:] * (lam * inv_l[:, tq:])).T
    on = o * lax.rsqrt(jnp.mean(o * o, axis=-1, keepdims=True) + NORM_EPS) * nw_ref[...] * (1.0 - lam_init)
    z = z_ref[...].astype(F32)
    o_ref[...] = (on * (z * _sigmoid(z))).astype(o_ref.dtype)


def _diff_attn(proj, lam_vecs, nw, *, batch, seq, lam_init, tq=1024, tk=256):
    m = batch * seq
    nq = seq // tq
    slopes = 2.0 ** (-8.0 * jnp.arange(1, DIFF_HEADS + 1, dtype=F32) / DIFF_HEADS)
    slopes = jnp.broadcast_to(slopes[:, None, None], (DIFF_HEADS, 1, LANES))
    return pl.pallas_call(
        functools.partial(_diff_kernel, tq=tq, tk=tk, lam_init=lam_init),
        grid=(batch, DIFF_HEADS, nq),
        in_specs=[
            pl.BlockSpec((4, DIFF_DH), lambda b, h, i: (0, 0)),
            pl.BlockSpec((1, DIFF_DV), lambda b, h, i: (0, 0)),
            pl.BlockSpec((1, 1, LANES), lambda b, h, i: (h, 0, 0)),
            pl.BlockSpec((tq, DIFF_DV), lambda b, h, i: (b * nq + i, COL_DIFF_Q + h)),
            pl.BlockSpec((seq, DIFF_DV), lambda b, h, i: (b, COL_DIFF_K + h)),
            pl.BlockSpec((seq, DIFF_DV), lambda b, h, i: (b, COL_DIFF_V + h)),
            pl.BlockSpec((tq, DIFF_DV), lambda b, h, i: (b * nq + i, COL_DIFF_Z + h)),
        ],
        out_specs=pl.BlockSpec((tq, DIFF_DV), lambda b, h, i: (b * nq + i, h)),
        out_shape=jax.ShapeDtypeStruct((m, DIFF_HEADS * DIFF_DV), BF16),
        scratch_shapes=[pltpu.VMEM((DIFF_DV + ONES_ROWS, seq), BF16),
                        pltpu.VMEM((DIFF_DV + ONES_ROWS, 2 * tq), F32),
                        ] + [pltpu.VMEM((tk, 2 * tq), F32)] * (tq // tk),
        name="diffattn",
        compiler_params=pltpu.CompilerParams(
            dimension_semantics=("arbitrary", "arbitrary", "arbitrary"), vmem_limit_bytes=VMEM_LIMIT),
    )(lam_vecs, nw, slopes, proj, proj, proj, proj)


def _memkv_kernel(mem_ref, nw_ref, w_ref, o_ref):
    x = mem_ref[...]
    ms = jnp.mean(x * x, axis=-1, keepdims=True)
    hb = ((x * lax.rsqrt(ms + NORM_EPS)) * nw_ref[...]).astype(BF16)
    o_ref[...] = _dot(hb, w_ref[...]).astype(o_ref.dtype)


def _memkv(mem2d, nw, w, *, tn=1024):
    rows = mem2d.shape[0]
    width = w.shape[1]
    return pl.pallas_call(
        _memkv_kernel,
        grid=(width // tn,),
        in_specs=[
            pl.BlockSpec((rows, D_MODEL), lambda j: (0, 0)),
            pl.BlockSpec((1, D_MODEL), lambda j: (0, 0)),
            pl.BlockSpec((D_MODEL, tn), lambda j: (0, j)),
        ],
        out_specs=pl.BlockSpec((rows, tn), lambda j: (0, j)),
        out_shape=jax.ShapeDtypeStruct((rows, width), BF16),
        name="memkv",
        compiler_params=pltpu.CompilerParams(
            dimension_semantics=("arbitrary",), vmem_limit_bytes=VMEM_LIMIT),
    )(mem2d, nw, w)


def _merge_kernel(x_ref, oa_ref, ob_ref, mq_ref, mz_ref, gg_ref, gd_ref, gm_ref, mk_ref, mv_ref,
                  wg_ref, wd_ref, wm_ref, wo_ref, pw_ref, o_ref):
    heads = []
    for h in range(MEM_HEADS):
        sl = slice(h * MEM_DH, (h + 1) * MEM_DH)
        s = _dot_nt(mq_ref[:, sl], mk_ref[:, sl]) * (MEM_DH ** -0.5)
        p = jnp.exp(s - jnp.max(s, axis=-1, keepdims=True))
        p = p / jnp.sum(p, axis=-1, keepdims=True)
        heads.append(_dot(p.astype(BF16), mv_ref[:, sl]))
    mz = mz_ref[...].astype(F32)
    oc = (jnp.concatenate(heads, axis=1) * (mz * _sigmoid(mz))).astype(BF16)

    y = _sigmoid(gg_ref[...].astype(F32)) * _dot(oa_ref[...], wg_ref[...])
    y = y + _sigmoid(gd_ref[...].astype(F32)) * _dot(ob_ref[...], wd_ref[...])
    y = y + _sigmoid(gm_ref[...].astype(F32)) * _dot(oc, wm_ref[...])
    out = _dot(y.astype(BF16), wo_ref[...])
    ms = jnp.mean(out * out, axis=-1, keepdims=True)
    o_ref[...] = x_ref[...] + (out * lax.rsqrt(ms + NORM_EPS)) * pw_ref[...]


def _merge(x, proj, oa, ob, memkv, wg, wd, wm, wo, pw, *, seq, mem_len, tm=512):
    m = x.shape[0]
    per_b = seq // tm
    mem_w = MEM_HEADS * MEM_DH
    tok = lambda width, col: pl.BlockSpec((tm, width), lambda i, col=col: (i, col))
    wspec = pl.BlockSpec((D_MODEL, D_MODEL), lambda i: (0, 0))
    return pl.pallas_call(
        _merge_kernel,
        grid=(m // tm,),
        in_specs=[
            tok(D_MODEL, 0), tok(D_MODEL, 0), tok(D_MODEL, 0),
            tok(mem_w, COL_MEM_Q), tok(mem_w, COL_MEM_Z),
            tok(D_MODEL, COL_GATES), tok(D_MODEL, COL_GATES + 1), tok(D_MODEL, COL_GATES + 2),
            pl.BlockSpec((mem_len, mem_w), lambda i: (i // per_b, 0)),
            pl.BlockSpec((mem_len, mem_w), lambda i: (i // per_b, 1)),
            wspec, wspec, wspec, wspec,
            pl.BlockSpec((1, D_MODEL), lambda i: (0, 0)),
        ],
        out_specs=pl.BlockSpec((tm, D_MODEL), lambda i: (i, 0)),
        out_shape=jax.ShapeDtypeStruct((m, D_MODEL), F32),
        name="merge",
        compiler_params=pltpu.CompilerParams(
            dimension_semantics=("arbitrary",), vmem_limit_bytes=VMEM_LIMIT),
    )(x, oa, ob, proj, proj, proj, proj, proj, memkv, memkv, wg, wd, wm, wo, pw)


def kernel(x, mem, pre_norm_w, post_norm_w, w_in, gdn_conv_w, gdn_a_log, gdn_dt_bias, gdn_norm_w,
           diff_lambda, diff_norm_w, mem_norm_w, w_mem_kv, w_br_gdn, w_br_diff, w_br_mem, w_out):
    batch, seq, d = x.shape
    mem_len = mem.shape[1]
    depth = w_in.shape[0]
    assert d == D_MODEL and seq % 1024 == 0

    w_a = w_in[:, :, QKV_W:QKV_W + GDN_HEADS]
    w_b = w_in[:, :, QKV_W + GDN_HEADS:QKV_W + 2 * GDN_HEADS]
    pad = jnp.zeros((depth, D_MODEL, LANES - GDN_HEADS), w_in.dtype)
    w_ab = jnp.concatenate([w_a, pad, w_b, pad], axis=-1).astype(BF16)
    hw = GDN_HEADS * GDN_D
    conv_w = gdn_conv_w.reshape(depth, GDN_CONV, 3, hw).transpose(0, 2, 1, 3)
    lane_pad = lambda t: jnp.pad(t, ((0, 0), (0, LANES - GDN_HEADS)))[:, None, :]
    a_log = lane_pad(gdn_a_log)
    dt_bias = lane_pad(gdn_dt_bias)
    w_mem_kv_b = w_mem_kv.astype(BF16)
    w_g, w_d, w_m, w_o = (t.astype(BF16) for t in (w_br_gdn, w_br_diff, w_br_mem, w_out))

    xf = x.reshape(batch * seq, d)
    mem2d = mem.reshape(batch * mem_len, d)
    for l in range(depth):
        lam_init = 0.8 - 0.6 * math.exp(-0.3 * l)
        proj, ab = _inproj(xf, pre_norm_w[l][None], w_in, l, w_ab[l])
        oa = _gdn(proj, ab, conv_w[l], a_log[l], dt_bias[l], gdn_norm_w[l][None], batch=batch, seq=seq)
        ob = _diff_attn(proj, diff_lambda[l], diff_norm_w[l][None], batch=batch, seq=seq, lam_init=lam_init)
        memkv = _memkv(mem2d, mem_norm_w[l][None], w_mem_kv_b[l])
        xf = _merge(xf, proj, oa, ob, memkv, w_g[l], w_d[l], w_m[l], w_o[l], post_norm_w[l][None],
                    seq=seq, mem_len=mem_len)
    return xf.reshape(batch, seq, d)
```

```python
import functools
import math

import jax
import jax.numpy as jnp
from jax import lax
from jax.experimental import pallas as pl
from jax.experimental.pallas import tpu as pltpu

F32 = jnp.float32
BF16 = jnp.bfloat16

D_MODEL = 1024
NORM_EPS = 1e-6
GDN_HEADS = 8
GDN_D = 128
GDN_CONV = 4
GDN_CHUNK = 64
DIFF_HEADS = 8
DIFF_DH = 64
DIFF_DV = 2 * DIFF_DH
MEM_HEADS = 4
MEM_DH = 256
N_BRANCH = 3

LANES = 128
LOG2E = 1.4426950408889634
ONES_ROWS = 16
LOOP_BLOCKS = 4
QKV_W = 3 * GDN_HEADS * GDN_D
COL_GDN_Q, COL_GDN_K, COL_GDN_V, COL_GDN_Z = 0, 1, 2, 3
COL_DIFF_Q, COL_DIFF_K, COL_DIFF_V, COL_DIFF_Z = 32, 40, 48, 56
COL_MEM_Q, COL_MEM_Z = 8, 9
COL_GATES = 10
PROJ_W = 13 * D_MODEL
AB_W = 2 * LANES
AB_SKIP = 2 * GDN_HEADS

VMEM_LIMIT = 56 * 1024 * 1024


def _sigmoid(x):
    return 1.0 / (1.0 + jnp.exp(-x))


def _softplus(x):
    return jnp.maximum(x, 0.0) + jnp.log(1.0 + jnp.exp(-jnp.abs(x)))


def _dot(a, b):
    return jnp.dot(a, b, preferred_element_type=F32)


def _dot_nt(a, b):
    return lax.dot_general(a, b, (((1,), (1,)), ((), ())), preferred_element_type=F32)


def _dot_tn(a, b):
    return lax.dot_general(a, b, (((0,), (0,)), ((), ())), preferred_element_type=F32)


def _inproj_kernel(x_ref, nw_ref, wt_ref, wabt_ref, o_ref, oab_ref, h_scr):
    j = pl.program_id(1)

    @pl.when(j == 0)
    def _():
        x = x_ref[...]
        ms = jnp.mean(x * x, axis=-1, keepdims=True)
        hb = ((x * lax.rsqrt(ms + NORM_EPS)) * nw_ref[...]).astype(BF16)
        h_scr[...] = hb
        pad = jnp.zeros((LANES - GDN_HEADS, D_MODEL), F32)
        wab_t = jnp.concatenate([wabt_ref[0, 0:GDN_HEADS], pad, wabt_ref[0, GDN_HEADS:AB_SKIP], pad], axis=0)
        oab_ref[...] = _dot_nt(hb, wab_t.astype(BF16))

    o_ref[...] = _dot_nt(h_scr[...], wt_ref[0].astype(BF16)).astype(o_ref.dtype)


def _inproj(x, nw, w_in_t, layer, *, tm=2048, tn=1024):
    m = x.shape[0]
    tm = min(tm, m)
    n_qkv = QKV_W // tn
    assert n_qkv * tn == QKV_W and w_in_t.shape[1] == PROJ_W + AB_SKIP
    return pl.pallas_call(
        _inproj_kernel,
        grid=(m // tm, PROJ_W // tn),
        in_specs=[
            pl.BlockSpec((tm, D_MODEL), lambda i, j: (i, 0)),
            pl.BlockSpec((1, D_MODEL), lambda i, j: (0, 0)),
            pl.BlockSpec((pl.Element(1), pl.Element(tn), pl.Element(D_MODEL)),
                         lambda i, j: (layer, pl.multiple_of(j * tn + jnp.where(j >= n_qkv, AB_SKIP, 0), 8), 0)),
            pl.BlockSpec((pl.Element(1), pl.Element(AB_SKIP), pl.Element(D_MODEL)),
                         lambda i, j: (layer, QKV_W, 0)),
        ],
        out_specs=[
            pl.BlockSpec((tm, tn), lambda i, j: (i, j)),
            pl.BlockSpec((tm, AB_W), lambda i, j: (i, 0)),
        ],
        out_shape=[
            jax.ShapeDtypeStruct((m, PROJ_W), BF16),
            jax.ShapeDtypeStruct((m, AB_W), F32),
        ],
        scratch_shapes=[pltpu.VMEM((tm, D_MODEL), BF16)],
        name="inproj",
        compiler_params=pltpu.CompilerParams(
            dimension_semantics=("arbitrary", "arbitrary"), vmem_limit_bytes=VMEM_LIMIT),
    )(x, nw, w_in_t, w_in_t)


def _gdn_kernel(q_ref, k_ref, v_ref, z_ref, a_ref, b_ref, cw_ref, alog_ref, dtb_ref, nw_ref, o_ref,
                state_scr, xpad, qs, ks, vs, gcol, bcol, grow, brow, us, wqe, attn_s, kd_s, *, t_blk):
    c_len = GDN_CHUNK
    n_chunk = t_blk // c_len
    hw = GDN_HEADS * GDN_D

    @pl.when(pl.program_id(1) == 0)
    def _():
        state_scr[...] = jnp.zeros_like(state_scr)
        xpad[:, 0:8, :] = jnp.zeros((3, 8, hw), F32)

    def conv_silu(ref, idx):
        xpad[idx, 8:8 + t_blk, :] = ref[...].astype(F32)
        w = cw_ref[idx]
        y = w[GDN_CONV - 1:GDN_CONV] * xpad[idx, 8:8 + t_blk, :]
        for j in range(GDN_CONV - 1):
            y = y + w[j:j + 1] * xpad[idx, 5 + j:5 + j + t_blk, :]
        xpad[idx, 0:8, :] = xpad[idx, t_blk:t_blk + 8, :]
        return y * _sigmoid(y)

    yq = conv_silu(q_ref, 0)
    yk = conv_silu(k_ref, 1)
    vs[...] = conv_silu(v_ref, 2)
    for h in range(GDN_HEADS):
        sl = slice(h * GDN_D, (h + 1) * GDN_D)
        qh = yq[:, sl]
        kh = yk[:, sl]
        qs[:, sl] = qh * (lax.rsqrt(jnp.sum(qh * qh, axis=-1, keepdims=True) + NORM_EPS) * (GDN_D ** -0.5))
        ks[:, sl] = kh * lax.rsqrt(jnp.sum(kh * kh, axis=-1, keepdims=True) + NORM_EPS)

    g = -jnp.exp(alog_ref[...]) * _softplus(a_ref[...] + dtb_ref[...])
    beta = _sigmoid(b_ref[...])
    rin = lax.broadcasted_iota(jnp.int32, (t_blk, LANES), 0) & (c_len - 1)
    s = 1
    while s < c_len:
        g = g + jnp.where(rin >= s, pltpu.roll(g, s, axis=0), 0.0)
        s *= 2
    gcol[...] = g
    bcol[...] = beta
    g_t = g.T
    b_t = beta.T
    for c in range(n_chunk):
        cols = slice(c * c_len, (c + 1) * c_len)
        grow[c] = g_t[0:8, cols]
        brow[c] = jnp.concatenate([b_t[0:8, cols], b_t[0:8, cols]], axis=1)

    ii = lax.broadcasted_iota(jnp.int32, (c_len, c_len), 0)
    jj = lax.broadcasted_iota(jnp.int32, (c_len, c_len), 1)
    incl = ii >= jj
    strict = ii > jj
    eye = jnp.where(ii == jj, 1.0, 0.0).astype(F32)
    lane_lo = lax.broadcasted_iota(jnp.int32, (c_len, 2 * c_len), 1) < c_len
    zeros_w = jnp.zeros((c_len, 2 * c_len), BF16)
    zeros_rhs = jnp.zeros((c_len, 2 * GDN_D), BF16)
    head_sl = [slice(h * GDN_D, (h + 1) * GDN_D) for h in range(GDN_HEADS)]

    chains = []
    for c in range(n_chunk):
        rows = slice(c * c_len, (c + 1) * c_len)
        g_c = gcol[rows, :]
        b_c = bcol[rows, :]
        g_r = grow[c]
        b_r = brow[c]
        for h in range(GDN_HEADS):
            chains.append(dict(
                idx=c * GDN_HEADS + h, rows=rows, sl=head_sl[h],
                q=qs[rows, head_sl[h]], k=ks[rows, head_sl[h]], v=vs[rows, head_sl[h]],
                gi=g_c[:, h:h + 1], gj=g_r[h:h + 1, :], bi=b_c[:, h:h + 1], bj2=b_r[h:h + 1, :],
                g_last=g_c[c_len - 1:c_len, h:h + 1]))
    for x in chains:
        kb = x["k"].astype(BF16)
        x["qk"] = _dot_nt(jnp.concatenate([x["q"].astype(BF16), kb], axis=0), kb)
    for x in chains:
        gam = jnp.exp(jnp.where(incl, x["gi"] - x["gj"], -jnp.inf))
        x["attn"] = (x["qk"][:c_len] * gam).astype(BF16)
        a_mat = jnp.where(strict, x["bi"] * x["qk"][c_len:] * gam, 0.0)
        x["w"] = jnp.concatenate([-a_mat, eye], axis=1)
    for _ in range(6):
        for x in chains:
            wb = x["w"].astype(BF16)
            x["r"] = _dot(wb, jnp.concatenate([wb, zeros_w], axis=0))
        for x in chains:
            x["w"] = jnp.where(lane_lo, x["r"], x["w"] + x["r"])
    for x in chains:
        eg = jnp.exp(x["gi"])
        rhs = jnp.concatenate([x["v"], x["k"] * eg], axis=1).astype(BF16)
        t_mat = (x["w"] * x["bj2"]).astype(BF16)
        x["uw"] = _dot(t_mat, jnp.concatenate([zeros_rhs, rhs], axis=0))
        x["qe"] = (x["q"] * eg).astype(BF16)
        x["kd"] = (x["k"] * jnp.exp(x["g_last"] - x["gi"])).astype(BF16)
    for x in chains:
        us[x["rows"], x["sl"]] = x["uw"][:, :GDN_D]
        wqe[x["idx"], 0:c_len, :] = x["uw"][:, GDN_D:].astype(BF16)
        wqe[x["idx"], c_len:2 * c_len, :] = x["qe"]
        attn_s[x["idx"]] = x["attn"]
        kd_s[x["idx"]] = x["kd"]

    st = [state_scr[h] for h in range(GDN_HEADS)]
    for c in range(n_chunk):
        rows = slice(c * c_len, (c + 1) * c_len)
        g_c = gcol[rows, :]
        idx = [c * GDN_HEADS + h for h in range(GDN_HEADS)]
        wq = [_dot(wqe[idx[h]], st[h].astype(BF16)) for h in range(GDN_HEADS)]
        v_new = [(us[rows, head_sl[h]] - wq[h][:c_len]).astype(BF16) for h in range(GDN_HEADS)]
        o = [wq[h][c_len:] + _dot(attn_s[idx[h]], v_new[h]) for h in range(GDN_HEADS)]
        st = [st[h] * jnp.exp(g_c[c_len - 1:c_len, h:h + 1]) + _dot_tn(kd_s[idx[h]], v_new[h])
              for h in range(GDN_HEADS)]
        for h in range(GDN_HEADS):
            us[rows, head_sl[h]] = o[h]
    for h in range(GDN_HEADS):
        state_scr[h] = st[h]

    nw = nw_ref[...]
    for h in range(GDN_HEADS):
        o = us[:, head_sl[h]]
        on = o * lax.rsqrt(jnp.mean(o * o, axis=-1, keepdims=True) + NORM_EPS) * nw
        z = z_ref[:, head_sl[h]].astype(F32)
        o_ref[:, head_sl[h]] = (on * (z * _sigmoid(z))).astype(o_ref.dtype)


def _gdn(proj, ab, cw, alog, dtb, nw, *, batch, seq, t_blk=256):
    m = batch * seq
    hw = GDN_HEADS * GDN_D
    nt = seq // t_blk
    n_chunk = t_blk // GDN_CHUNK
    tok = lambda col: pl.BlockSpec((t_blk, hw), lambda b, t, col=col: (b * nt + t, col))
    const2 = lambda shape: pl.BlockSpec(shape, lambda b, t: (0,) * len(shape))
    return pl.pallas_call(
        functools.partial(_gdn_kernel, t_blk=t_blk),
        grid=(batch, nt),
        in_specs=[
            tok(COL_GDN_Q), tok(COL_GDN_K), tok(COL_GDN_V), tok(COL_GDN_Z),
            pl.BlockSpec((t_blk, LANES), lambda b, t: (b * nt + t, 0)),
            pl.BlockSpec((t_blk, LANES), lambda b, t: (b * nt + t, 1)),
            const2((3, GDN_CONV, hw)), const2((1, LANES)), const2((1, LANES)), const2((1, GDN_D)),
        ],
        out_specs=pl.BlockSpec((t_blk, hw), lambda b, t: (b * nt + t, 0)),
        out_shape=jax.ShapeDtypeStruct((m, hw), BF16),
        scratch_shapes=[
            pltpu.VMEM((GDN_HEADS, GDN_D, GDN_D), F32),
            pltpu.VMEM((3, 8 + t_blk, hw), F32),
            pltpu.VMEM((t_blk, hw), F32), pltpu.VMEM((t_blk, hw), F32), pltpu.VMEM((t_blk, hw), F32),
            pltpu.VMEM((t_blk, LANES), F32), pltpu.VMEM((t_blk, LANES), F32),
            pltpu.VMEM((n_chunk, 8, GDN_CHUNK), F32), pltpu.VMEM((n_chunk, 8, 2 * GDN_CHUNK), F32),
            pltpu.VMEM((t_blk, hw), F32),
            pltpu.VMEM((n_chunk * GDN_HEADS, 2 * GDN_CHUNK, GDN_D), BF16),
            pltpu.VMEM((n_chunk * GDN_HEADS, GDN_CHUNK, GDN_CHUNK), BF16),
            pltpu.VMEM((n_chunk * GDN_HEADS, GDN_CHUNK, GDN_D), BF16),
        ],
        name="gdn",
        compiler_params=pltpu.CompilerParams(
            dimension_semantics=("arbitrary", "arbitrary"), vmem_limit_bytes=VMEM_LIMIT),
    )(proj, proj, proj, proj, ab, ab, cw, alog, dtb, nw)


def _diff_kernel(lam_ref, nw_ref, slope_ref, q_ref, k_ref, v_ref, z_ref, o_ref, vt_scr, acc_scr,
                 s0_scr, s1_scr, *sd_scr, tq, tk, lam_init):
    qi = pl.program_id(2)
    n_sub = tq // tk
    assert n_sub * tk == tq and n_sub % LOOP_BLOCKS == 0 and LOOP_BLOCKS % 2 == 0

    @pl.when(qi == 0)
    def _():
        vt_scr[0:DIFF_DV, :] = v_ref[...].astype(F32).T.astype(BF16)
        vt_scr[DIFF_DV:, :] = jnp.ones((ONES_ROWS, vt_scr.shape[1]), BF16)

    q_t = (q_ref[...].astype(F32) * (DIFF_DH ** -0.5 * LOG2E)).T
    row = lax.broadcasted_iota(jnp.int32, (2 * DIFF_DH, tq), 0)
    qq = jnp.concatenate([jnp.where(row < DIFF_DH, q_t, 0.0), jnp.where(row >= DIFF_DH, q_t, 0.0)],
                         axis=1).astype(BF16)
    slope = slope_ref[0]
    key_pos = lax.broadcasted_iota(jnp.int32, (tk, LANES), 0).astype(F32)
    lane = lax.broadcasted_iota(jnp.int32, (tk, LANES), 1)
    b_full = slope * key_pos
    b_hi = b_full.astype(BF16).astype(F32)
    b_lo = b_full - b_hi
    k_bias = jnp.where(lane < 2, b_hi, jnp.where(lane < 4, b_lo, 0.0)).astype(BF16)
    c_full = jnp.full((LANES, 2 * tq), LOG2E, F32)
    c_hi = c_full.astype(BF16).astype(F32)
    c_lo = c_full - c_hi
    row2 = lax.broadcasted_iota(jnp.int32, (LANES, 2 * tq), 0)
    q_bias = jnp.where(row2 >= 4, 0.0, jnp.where((row2 & 1) == 0, c_hi, c_lo)).astype(BF16)
    qq = jnp.concatenate([qq, q_bias], axis=0)
    slope_l2 = slope[:, 0:1] * LOG2E
    acc_scr[...] = jnp.zeros_like(acc_scr)

    slabs = [(slice(c * tk, (c + 1) * tk), c % n_sub) for c in range(2 * n_sub)]

    def scores(kj, s_ref, sub):
        k0 = pl.multiple_of(kj * tk, tk)
        ka = jnp.concatenate([k_ref[pl.ds(k0, tk), :], k_bias], axis=1)
        if sub is None:
            s_ref[...] = _dot(ka, qq)
        else:
            for cs, pos in slabs:
                if pos >= sub:
                    s_ref[:, cs] = _dot(ka, qq[:, cs])

    def consume(kj, s_ref, m):
        k0 = pl.multiple_of(kj * tk, tk)
        off = slope_l2 * (kj * tk).astype(F32)
        vt = vt_scr[:, pl.ds(k0, tk)]
        m_out = []
        for cs, _ in slabs:
            s = s_ref[:, cs]
            m_new = jnp.maximum(m[:, cs], jnp.max(s, axis=0, keepdims=True) + off)
            p = jnp.exp2(s - (m_new - off))
            alpha = jnp.exp2(m[:, cs] - m_new)
            m_out.append(m_new)
            acc_scr[:, cs] = acc_scr[:, cs] * alpha + _dot(vt, p.astype(BF16))
        return jnp.concatenate(m_out, axis=1)

    def group(j, m):
        for u in range(0, LOOP_BLOCKS, 2):
            kj = LOOP_BLOCKS * j + u
            scores(kj + 1, s1_scr, None)
            m = consume(kj, s0_scr, m)
            scores(kj + 2, s0_scr, None)
            m = consume(kj + 1, s1_scr, m)
        return m

    m0 = jnp.full((1, 2 * tq), -jnp.inf, F32)
    scores(0, s0_scr, None)
    n_full = qi * n_sub
    m = lax.fori_loop(0, n_full // LOOP_BLOCKS, group, m0)

    dbufs = (s0_scr, s1_scr) + tuple(sd_scr)
    for sub in range(1, n_sub):
        scores(n_full + sub, dbufs[sub], sub)
    causal = lax.broadcasted_iota(jnp.int32, (tk, tk), 0) <= lax.broadcasted_iota(jnp.int32, (tk, tk), 1)
    offs = [slope_l2 * ((n_full + sub) * tk).astype(F32) for sub in range(n_sub)]
    for cs, pos in slabs:
        s_blk = [dbufs[sub][:, cs] for sub in range(pos)] + [jnp.where(causal, dbufs[pos][:, cs], -jnp.inf)]
        m_old = m[:, cs]
        m_new = m_old
        for sub in range(pos + 1):
            m_new = jnp.maximum(m_new, jnp.max(s_blk[sub], axis=0, keepdims=True) + offs[sub])
        pv = None
        for sub in range(pos + 1):
            p = jnp.exp2(s_blk[sub] - (m_new - offs[sub])).astype(BF16)
            k0 = pl.multiple_of((n_full + sub) * tk, tk)
            d = _dot(vt_scr[:, pl.ds(k0, tk)], p)
            pv = d if pv is None else pv + d
        acc_scr[:, cs] = acc_scr[:, cs] * jnp.exp2(m_old - m_new) + pv

    lv = lam_ref[...]
    lam = (jnp.exp(jnp.sum(lv[0:1] * lv[1:2], axis=-1, keepdims=True))
           - jnp.exp(jnp.sum(lv[2:3] * lv[3:4], axis=-1, keepdims=True)) + lam_init)
    inv_l = 1.0 / acc_scr[DIFF_DV:DIFF_DV + 1, :]
    o = (acc_scr[0:DIFF_DV, 0:tq] * inv_l[:, :tq] - acc_scr[0:DIFF_DV, tq:] * (lam * inv_l[:, tq:])).T
    on = o * lax.rsqrt(jnp.mean(o * o, axis=-1, keepdims=True) + NORM_EPS) * nw_ref[...] * (1.0 - lam_init)
    z = z_ref[...].astype(F32)
    o_ref[...] = (on * (z * _sigmoid(z))).astype(o_ref.dtype)


def _diff_attn(proj, lam_vecs, nw, *, batch, seq, lam_init, tq=1024, tk=256):
    m = batch * seq
    nq = seq // tq
    slopes = 2.0 ** (-8.0 * jnp.arange(1, DIFF_HEADS + 1, dtype=F32) / DIFF_HEADS)
    slopes = jnp.broadcast_to(slopes[:, None, None], (DIFF_HEADS, 1, LANES))
    return pl.pallas_call(
        functools.partial(_diff_kernel, tq=tq, tk=tk, lam_init=lam_init),
        grid=(batch, DIFF_HEADS, nq),
        in_specs=[
            pl.BlockSpec((4, DIFF_DH), lambda b, h, i: (0, 0)),
            pl.BlockSpec((1, DIFF_DV), lambda b, h, i: (0, 0)),
            pl.BlockSpec((1, 1, LANES), lambda b, h, i: (h, 0, 0)),
            pl.BlockSpec((tq, DIFF_DV), lambda b, h, i: (b * nq + i, COL_DIFF_Q + h)),
            pl.BlockSpec((seq, DIFF_DV), lambda b, h, i: (b, COL_DIFF_K + h)),
            pl.BlockSpec((seq, DIFF_DV), lambda b, h, i: (b, COL_DIFF_V + h)),
            pl.BlockSpec((tq, DIFF_DV), lambda b, h, i: (b * nq + i, COL_DIFF_Z + h)),
        ],
        out_specs=pl.BlockSpec((tq, DIFF_DV), lambda b, h, i: (b * nq + i, h)),
        out_shape=jax.ShapeDtypeStruct((m, DIFF_HEADS * DIFF_DV), BF16),
        scratch_shapes=[pltpu.VMEM((DIFF_DV + ONES_ROWS, seq), BF16),
                        pltpu.VMEM((DIFF_DV + ONES_ROWS, 2 * tq), F32),
                        ] + [pltpu.VMEM((tk, 2 * tq), F32)] * (tq // tk),
        name="diffattn",
        compiler_params=pltpu.CompilerParams(
            dimension_semantics=("arbitrary", "arbitrary", "arbitrary"), vmem_limit_bytes=VMEM_LIMIT),
    )(lam_vecs, nw, slopes, proj, proj, proj, proj)


def _memkv_kernel(mem_ref, nw_ref, w_ref, o_ref):
    x = mem_ref[...]
    ms = jnp.mean(x * x, axis=-1, keepdims=True)
    hb = ((x * lax.rsqrt(ms + NORM_EPS)) * nw_ref[...]).astype(BF16)
    o_ref[...] = _dot(hb, w_ref[...]).astype(o_ref.dtype)


def _memkv(mem2d, nw, w, *, tn=1024):
    rows = mem2d.shape[0]
    width = w.shape[1]
    return pl.pallas_call(
        _memkv_kernel,
        grid=(width // tn,),
        in_specs=[
            pl.BlockSpec((rows, D_MODEL), lambda j: (0, 0)),
            pl.BlockSpec((1, D_MODEL), lambda j: (0, 0)),
            pl.BlockSpec((D_MODEL, tn), lambda j: (0, j)),
        ],
        out_specs=pl.BlockSpec((rows, tn), lambda j: (0, j)),
        out_shape=jax.ShapeDtypeStruct((rows, width), BF16),
        name="memkv",
        compiler_params=pltpu.CompilerParams(
            dimension_semantics=("arbitrary",), vmem_limit_bytes=VMEM_LIMIT),
    )(mem2d, nw, w)


def _merge_kernel(x_ref, oa_ref, ob_ref, mq_ref, mz_ref, gg_ref, gd_ref, gm_ref, mk_ref, mv_ref,
                  wg_ref, wd_ref, wm_ref, wo_ref, pw_ref, o_ref):
    heads = []
    for h in range(MEM_HEADS):
        sl = slice(h * MEM_DH, (h + 1) * MEM_DH)
        s = _dot_nt(mq_ref[:, sl], mk_ref[:, sl]) * (MEM_DH ** -0.5)
        p = jnp.exp(s - jnp.max(s, axis=-1, keepdims=True))
        p = p / jnp.sum(p, axis=-1, keepdims=True)
        heads.append(_dot(p.astype(BF16), mv_ref[:, sl]))
    mz = mz_ref[...].astype(F32)
    oc = (jnp.concatenate(heads, axis=1) * (mz * _sigmoid(mz))).astype(BF16)

    y = _sigmoid(gg_ref[...].astype(F32)) * _dot(oa_ref[...], wg_ref[...])
    y = y + _sigmoid(gd_ref[...].astype(F32)) * _dot(ob_ref[...], wd_ref[...])
    y = y + _sigmoid(gm_ref[...].astype(F32)) * _dot(oc, wm_ref[...])
    out = _dot(y.astype(BF16), wo_ref[...])
    ms = jnp.mean(out * out, axis=-1, keepdims=True)
    o_ref[...] = x_ref[...] + (out * lax.rsqrt(ms + NORM_EPS)) * pw_ref[...]


def _merge(x, proj, oa, ob, memkv, wg, wd, wm, wo, pw, *, seq, mem_len, tm=512):
    m = x.shape[0]
    per_b = seq // tm
    mem_w = MEM_HEADS * MEM_DH
    tok = lambda width, col: pl.BlockSpec((tm, width), lambda i, col=col: (i, col))
    wspec = pl.BlockSpec((D_MODEL, D_MODEL), lambda i: (0, 0))
    return pl.pallas_call(
        _merge_kernel,
        grid=(m // tm,),
        in_specs=[
            tok(D_MODEL, 0), tok(D_MODEL, 0), tok(D_MODEL, 0),
            tok(mem_w, COL_MEM_Q), tok(mem_w, COL_MEM_Z),
            tok(D_MODEL, COL_GATES), tok(D_MODEL, COL_GATES + 1), tok(D_MODEL, COL_GATES + 2),
            pl.BlockSpec((mem_len, mem_w), lambda i: (i // per_b, 0)),
            pl.BlockSpec((mem_len, mem_w), lambda i: (i // per_b, 1)),
            wspec, wspec, wspec, wspec,
            pl.BlockSpec((1, D_MODEL), lambda i: (0, 0)),
        ],
        out_specs=pl.BlockSpec((tm, D_MODEL), lambda i: (i, 0)),
        out_shape=jax.ShapeDtypeStruct((m, D_MODEL), F32),
        name="merge",
        compiler_params=pltpu.CompilerParams(
            dimension_semantics=("arbitrary",), vmem_limit_bytes=VMEM_LIMIT),
    )(x, oa, ob, proj, proj, proj, proj, proj, memkv, memkv, wg, wd, wm, wo, pw)


def kernel(x, mem, pre_norm_w, post_norm_w, w_in, gdn_conv_w, gdn_a_log, gdn_dt_bias, gdn_norm_w,
           diff_lambda, diff_norm_w, mem_norm_w, w_mem_kv, w_br_gdn, w_br_diff, w_br_mem, w_out):
    batch, seq, d = x.shape
    mem_len = mem.shape[1]
    depth = w_in.shape[0]
    assert d == D_MODEL and seq % 1024 == 0

    w_in_t = jnp.swapaxes(w_in, 1, 2)
    hw = GDN_HEADS * GDN_D
    conv_w = gdn_conv_w.reshape(depth, GDN_CONV, 3, hw).transpose(0, 2, 1, 3)
    lane_pad = lambda t: jnp.pad(t, ((0, 0), (0, LANES - GDN_HEADS)))[:, None, :]
    a_log = lane_pad(gdn_a_log)
    dt_bias = lane_pad(gdn_dt_bias)
    w_mem_kv_b = w_mem_kv.astype(BF16)
    w_g, w_d, w_m, w_o = (t.astype(BF16) for t in (w_br_gdn, w_br_diff, w_br_mem, w_out))

    xf = x.reshape(batch * seq, d)
    mem2d = mem.reshape(batch * mem_len, d)
    for l in range(depth):
        lam_init = 0.8 - 0.6 * math.exp(-0.3 * l)
        proj, ab = _inproj(xf, pre_norm_w[l][None], w_in_t, l)
        oa = _gdn(proj, ab, conv_w[l], a_log[l], dt_bias[l], gdn_norm_w[l][None], batch=batch, seq=seq)
        ob = _diff_attn(proj, diff_lambda[l], diff_norm_w[l][None], batch=batch, seq=seq, lam_init=lam_init)
        memkv = _memkv(mem2d, mem_norm_w[l][None], w_mem_kv_b[l])
        xf = _merge(xf, proj, oa, ob, memkv, w_g[l], w_d[l], w_m[l], w_o[l], post_norm_w[l][None],
                    seq=seq, mem_len=mem_len)
    return xf.reshape(batch, seq, d)
```

```python
import functools
import math

import jax
import jax.numpy as jnp
from jax import lax
from jax.experimental import pallas as pl
from jax.experimental.pallas import tpu as pltpu

F32 = jnp.float32
BF16 = jnp.bfloat16

D_MODEL = 1024
NORM_EPS = 1e-6
GDN_HEADS = 8
GDN_D = 128
GDN_CONV = 4
GDN_CHUNK = 64
DIFF_HEADS = 8
DIFF_DH = 64
DIFF_DV = 2 * DIFF_DH
MEM_HEADS = 4
MEM_DH = 256
N_BRANCH = 3

LANES = 128
LOG2E = 1.4426950408889634
ONES_ROWS = 16
LOOP_BLOCKS = 8
QKV_W = 3 * GDN_HEADS * GDN_D
COL_GDN_Q, COL_GDN_K, COL_GDN_V, COL_GDN_Z = 0, 1, 2, 3
COL_DIFF_Q, COL_DIFF_K, COL_DIFF_V, COL_DIFF_Z = 32, 40, 48, 56
COL_MEM_Q, COL_MEM_Z = 8, 9
COL_GATES = 10
PROJ_W = 13 * D_MODEL
AB_W = 2 * LANES
AB_SKIP = 2 * GDN_HEADS

VMEM_LIMIT = 56 * 1024 * 1024


def _sigmoid(x):
    return 1.0 / (1.0 + jnp.exp(-x))


def _softplus(x):
    return jnp.maximum(x, 0.0) + jnp.log(1.0 + jnp.exp(-jnp.abs(x)))


def _dot(a, b):
    return jnp.dot(a, b, preferred_element_type=F32)


def _dot_nt(a, b):
    return lax.dot_general(a, b, (((1,), (1,)), ((), ())), preferred_element_type=F32)


def _dot_tn(a, b):
    return lax.dot_general(a, b, (((0,), (0,)), ((), ())), preferred_element_type=F32)


def _inproj_kernel(x_ref, nw_ref, wt_ref, wabt_ref, o_ref, oab_ref, h_scr):
    j = pl.program_id(1)

    @pl.when(j == 0)
    def _():
        x = x_ref[...]
        ms = jnp.mean(x * x, axis=-1, keepdims=True)
        hb = ((x * lax.rsqrt(ms + NORM_EPS)) * nw_ref[...]).astype(BF16)
        h_scr[...] = hb
        pad = jnp.zeros((LANES - GDN_HEADS, D_MODEL), F32)
        wab_t = jnp.concatenate([wabt_ref[0, 0:GDN_HEADS], pad, wabt_ref[0, GDN_HEADS:AB_SKIP], pad], axis=0)
        oab_ref[...] = _dot_nt(hb, wab_t.astype(BF16))

    o_ref[...] = _dot_nt(h_scr[...], wt_ref[0].astype(BF16)).astype(o_ref.dtype)


def _inproj(x, nw, w_in_t, layer, *, tm=2048, tn=1024):
    m = x.shape[0]
    tm = min(tm, m)
    n_qkv = QKV_W // tn
    assert n_qkv * tn == QKV_W and w_in_t.shape[1] == PROJ_W + AB_SKIP
    return pl.pallas_call(
        _inproj_kernel,
        grid=(m // tm, PROJ_W // tn),
        in_specs=[
            pl.BlockSpec((tm, D_MODEL), lambda i, j: (i, 0)),
            pl.BlockSpec((1, D_MODEL), lambda i, j: (0, 0)),
            pl.BlockSpec((pl.Element(1), pl.Element(tn), pl.Element(D_MODEL)),
                         lambda i, j: (layer, pl.multiple_of(j * tn + jnp.where(j >= n_qkv, AB_SKIP, 0), 8), 0)),
            pl.BlockSpec((pl.Element(1), pl.Element(AB_SKIP), pl.Element(D_MODEL)),
                         lambda i, j: (layer, QKV_W, 0)),
        ],
        out_specs=[
            pl.BlockSpec((tm, tn), lambda i, j: (i, j)),
            pl.BlockSpec((tm, AB_W), lambda i, j: (i, 0)),
        ],
        out_shape=[
            jax.ShapeDtypeStruct((m, PROJ_W), BF16),
            jax.ShapeDtypeStruct((m, AB_W), F32),
        ],
        scratch_shapes=[pltpu.VMEM((tm, D_MODEL), BF16)],
        name="inproj",
        compiler_params=pltpu.CompilerParams(
            dimension_semantics=("arbitrary", "arbitrary"), vmem_limit_bytes=VMEM_LIMIT),
    )(x, nw, w_in_t, w_in_t)


def _gdn_kernel(q_ref, k_ref, v_ref, z_ref, a_ref, b_ref, cw_ref, alog_ref, dtb_ref, nw_ref, o_ref,
                state_scr, xpad, qs, ks, vs, gcol, bcol, grow, brow, us, wqe, attn_s, kd_s, *, t_blk):
    c_len = GDN_CHUNK
    n_chunk = t_blk // c_len
    hw = GDN_HEADS * GDN_D

    @pl.when(pl.program_id(1) == 0)
    def _():
        state_scr[...] = jnp.zeros_like(state_scr)
        xpad[:, 0:8, :] = jnp.zeros((3, 8, hw), F32)

    def conv_silu(ref, idx):
        xpad[idx, 8:8 + t_blk, :] = ref[...].astype(F32)
        w = cw_ref[idx]
        y = w[GDN_CONV - 1:GDN_CONV] * xpad[idx, 8:8 + t_blk, :]
        for j in range(GDN_CONV - 1):
            y = y + w[j:j + 1] * xpad[idx, 5 + j:5 + j + t_blk, :]
        xpad[idx, 0:8, :] = xpad[idx, t_blk:t_blk + 8, :]
        return y * _sigmoid(y)

    yq = conv_silu(q_ref, 0)
    yk = conv_silu(k_ref, 1)
    vs[...] = conv_silu(v_ref, 2)
    for h in range(GDN_HEADS):
        sl = slice(h * GDN_D, (h + 1) * GDN_D)
        qh = yq[:, sl]
        kh = yk[:, sl]
        qs[:, sl] = qh * (lax.rsqrt(jnp.sum(qh * qh, axis=-1, keepdims=True) + NORM_EPS) * (GDN_D ** -0.5))
        ks[:, sl] = kh * lax.rsqrt(jnp.sum(kh * kh, axis=-1, keepdims=True) + NORM_EPS)

    g = -jnp.exp(alog_ref[...]) * _softplus(a_ref[...] + dtb_ref[...])
    beta = _sigmoid(b_ref[...])
    rin = lax.broadcasted_iota(jnp.int32, (t_blk, LANES), 0) & (c_len - 1)
    s = 1
    while s < c_len:
        g = g + jnp.where(rin >= s, pltpu.roll(g, s, axis=0), 0.0)
        s *= 2
    gcol[...] = g
    bcol[...] = beta
    g_t = g.T
    b_t = beta.T
    for c in range(n_chunk):
        cols = slice(c * c_len, (c + 1) * c_len)
        grow[c] = g_t[0:8, cols]
        brow[c] = jnp.concatenate([b_t[0:8, cols], b_t[0:8, cols]], axis=1)

    ii = lax.broadcasted_iota(jnp.int32, (c_len, c_len), 0)
    jj = lax.broadcasted_iota(jnp.int32, (c_len, c_len), 1)
    incl = ii >= jj
    strict = ii > jj
    eye = jnp.where(ii == jj, 1.0, 0.0).astype(F32)
    lane_lo = lax.broadcasted_iota(jnp.int32, (c_len, 2 * c_len), 1) < c_len
    zeros_w = jnp.zeros((c_len, 2 * c_len), BF16)
    zeros_rhs = jnp.zeros((c_len, 2 * GDN_D), BF16)
    head_sl = [slice(h * GDN_D, (h + 1) * GDN_D) for h in range(GDN_HEADS)]

    chains = []
    for c in range(n_chunk):
        rows = slice(c * c_len, (c + 1) * c_len)
        g_c = gcol[rows, :]
        b_c = bcol[rows, :]
        g_r = grow[c]
        b_r = brow[c]
        for h in range(GDN_HEADS):
            chains.append(dict(
                idx=c * GDN_HEADS + h, rows=rows, sl=head_sl[h],
                q=qs[rows, head_sl[h]], k=ks[rows, head_sl[h]], v=vs[rows, head_sl[h]],
                gi=g_c[:, h:h + 1], gj=g_r[h:h + 1, :], bi=b_c[:, h:h + 1], bj2=b_r[h:h + 1, :],
                g_last=g_c[c_len - 1:c_len, h:h + 1]))
    for x in chains:
        kb = x["k"].astype(BF16)
        x["qk"] = _dot_nt(jnp.concatenate([x["q"].astype(BF16), kb], axis=0), kb)
    for x in chains:
        gam = jnp.exp(jnp.where(incl, x["gi"] - x["gj"], -jnp.inf))
        x["attn"] = (x["qk"][:c_len] * gam).astype(BF16)
        a_mat = jnp.where(strict, x["bi"] * x["qk"][c_len:] * gam, 0.0)
        x["w"] = jnp.concatenate([-a_mat, eye], axis=1)
    for _ in range(6):
        for x in chains:
            wb = x["w"].astype(BF16)
            x["r"] = _dot(wb, jnp.concatenate([wb, zeros_w], axis=0))
        for x in chains:
            x["w"] = jnp.where(lane_lo, x["r"], x["w"] + x["r"])
    for x in chains:
        eg = jnp.exp(x["gi"])
        rhs = jnp.concatenate([x["v"], x["k"] * eg], axis=1).astype(BF16)
        t_mat = (x["w"] * x["bj2"]).astype(BF16)
        x["uw"] = _dot(t_mat, jnp.concatenate([zeros_rhs, rhs], axis=0))
        x["qe"] = (x["q"] * eg).astype(BF16)
        x["kd"] = (x["k"] * jnp.exp(x["g_last"] - x["gi"])).astype(BF16)
    for x in chains:
        us[x["rows"], x["sl"]] = x["uw"][:, :GDN_D]
        wqe[x["idx"], 0:c_len, :] = x["uw"][:, GDN_D:].astype(BF16)
        wqe[x["idx"], c_len:2 * c_len, :] = x["qe"]
        attn_s[x["idx"]] = x["attn"]
        kd_s[x["idx"]] = x["kd"]

    st = [state_scr[h] for h in range(GDN_HEADS)]
    for c in range(n_chunk):
        rows = slice(c * c_len, (c + 1) * c_len)
        g_c = gcol[rows, :]
        idx = [c * GDN_HEADS + h for h in range(GDN_HEADS)]
        wq = [_dot(wqe[idx[h]], st[h].astype(BF16)) for h in range(GDN_HEADS)]
        v_new = [(us[rows, head_sl[h]] - wq[h][:c_len]).astype(BF16) for h in range(GDN_HEADS)]
        o = [wq[h][c_len:] + _dot(attn_s[idx[h]], v_new[h]) for h in range(GDN_HEADS)]
        st = [st[h] * jnp.exp(g_c[c_len - 1:c_len, h:h + 1]) + _dot_tn(kd_s[idx[h]], v_new[h])
              for h in range(GDN_HEADS)]
        for h in range(GDN_HEADS):
            us[rows, head_sl[h]] = o[h]
    for h in range(GDN_HEADS):
        state_scr[h] = st[h]

    nw = nw_ref[...]
    for h in range(GDN_HEADS):
        o = us[:, head_sl[h]]
        on = o * lax.rsqrt(jnp.mean(o * o, axis=-1, keepdims=True) + NORM_EPS) * nw
        z = z_ref[:, head_sl[h]].astype(F32)
        o_ref[:, head_sl[h]] = (on * (z * _sigmoid(z))).astype(o_ref.dtype)


def _gdn(proj, ab, cw, alog, dtb, nw, *, batch, seq, t_blk=256):
    m = batch * seq
    hw = GDN_HEADS * GDN_D
    nt = seq // t_blk
    n_chunk = t_blk // GDN_CHUNK
    tok = lambda col: pl.BlockSpec((t_blk, hw), lambda b, t, col=col: (b * nt + t, col))
    const2 = lambda shape: pl.BlockSpec(shape, lambda b, t: (0,) * len(shape))
    return pl.pallas_call(
        functools.partial(_gdn_kernel, t_blk=t_blk),
        grid=(batch, nt),
        in_specs=[
            tok(COL_GDN_Q), tok(COL_GDN_K), tok(COL_GDN_V), tok(COL_GDN_Z),
            pl.BlockSpec((t_blk, LANES), lambda b, t: (b * nt + t, 0)),
            pl.BlockSpec((t_blk, LANES), lambda b, t: (b * nt + t, 1)),
            const2((3, GDN_CONV, hw)), const2((1, LANES)), const2((1, LANES)), const2((1, GDN_D)),
        ],
        out_specs=pl.BlockSpec((t_blk, hw), lambda b, t: (b * nt + t, 0)),
        out_shape=jax.ShapeDtypeStruct((m, hw), BF16),
        scratch_shapes=[
            pltpu.VMEM((GDN_HEADS, GDN_D, GDN_D), F32),
            pltpu.VMEM((3, 8 + t_blk, hw), F32),
            pltpu.VMEM((t_blk, hw), F32), pltpu.VMEM((t_blk, hw), F32), pltpu.VMEM((t_blk, hw), F32),
            pltpu.VMEM((t_blk, LANES), F32), pltpu.VMEM((t_blk, LANES), F32),
            pltpu.VMEM((n_chunk, 8, GDN_CHUNK), F32), pltpu.VMEM((n_chunk, 8, 2 * GDN_CHUNK), F32),
            pltpu.VMEM((t_blk, hw), F32),
            pltpu.VMEM((n_chunk * GDN_HEADS, 2 * GDN_CHUNK, GDN_D), BF16),
            pltpu.VMEM((n_chunk * GDN_HEADS, GDN_CHUNK, GDN_CHUNK), BF16),
            pltpu.VMEM((n_chunk * GDN_HEADS, GDN_CHUNK, GDN_D), BF16),
        ],
        name="gdn",
        compiler_params=pltpu.CompilerParams(
            dimension_semantics=("arbitrary", "arbitrary"), vmem_limit_bytes=VMEM_LIMIT),
    )(proj, proj, proj, proj, ab, ab, cw, alog, dtb, nw)


def _diff_kernel(lam_ref, nw_ref, slope_ref, q_ref, k_ref, v_ref, z_ref, o_ref, vt_scr, acc_scr,
                 s0_scr, s1_scr, *sd_scr, tq, tk, lam_init):
    qi = pl.program_id(2)
    n_sub = tq // tk
    assert n_sub * tk == tq and n_sub % (LOOP_BLOCKS // 2) == 0 and LOOP_BLOCKS % 4 == 0

    @pl.when(qi == 0)
    def _():
        vt_scr[0:DIFF_DV, :] = v_ref[...].astype(F32).T.astype(BF16)
        vt_scr[DIFF_DV:, :] = jnp.ones((ONES_ROWS, vt_scr.shape[1]), BF16)

    q_t = (q_ref[...].astype(F32) * (DIFF_DH ** -0.5 * LOG2E)).T
    row = lax.broadcasted_iota(jnp.int32, (2 * DIFF_DH, tq), 0)
    qq = jnp.concatenate([jnp.where(row < DIFF_DH, q_t, 0.0), jnp.where(row >= DIFF_DH, q_t, 0.0)],
                         axis=1).astype(BF16)
    slope = slope_ref[0]
    key_pos = lax.broadcasted_iota(jnp.int32, (tk, LANES), 0).astype(F32)
    lane = lax.broadcasted_iota(jnp.int32, (tk, LANES), 1)
    b_full = slope * key_pos
    b_hi = b_full.astype(BF16).astype(F32)
    b_lo = b_full - b_hi
    k_bias = jnp.where(lane < 2, b_hi, jnp.where(lane < 4, b_lo, 0.0)).astype(BF16)
    c_full = jnp.full((LANES, 2 * tq), LOG2E, F32)
    c_hi = c_full.astype(BF16).astype(F32)
    c_lo = c_full - c_hi
    row2 = lax.broadcasted_iota(jnp.int32, (LANES, 2 * tq), 0)
    q_bias = jnp.where(row2 >= 4, 0.0, jnp.where((row2 & 1) == 0, c_hi, c_lo)).astype(BF16)
    qq = jnp.concatenate([qq, q_bias], axis=0)
    slope_l2 = slope[:, 0:1] * LOG2E
    acc_scr[...] = jnp.zeros_like(acc_scr)

    slabs = [(slice(c * tk, (c + 1) * tk), c % n_sub) for c in range(2 * n_sub)]

    def scores(kj, s_ref, sub):
        k0 = pl.multiple_of(kj * tk, tk)
        ka = jnp.concatenate([k_ref[pl.ds(k0, tk), :], k_bias], axis=1)
        if sub is None:
            s_ref[...] = _dot(ka, qq)
        else:
            for cs, pos in slabs:
                if pos >= sub:
                    s_ref[:, cs] = _dot(ka, qq[:, cs])

    def consume(kj, s_ref, m):
        k0 = pl.multiple_of(kj * tk, tk)
        off = slope_l2 * (kj * tk).astype(F32)
        vt = vt_scr[:, pl.ds(k0, tk)]
        m_out = []
        for cs, _ in slabs:
            s = s_ref[:, cs]
            m_new = jnp.maximum(m[:, cs], jnp.max(s, axis=0, keepdims=True) + off)
            p = jnp.exp2(s - (m_new - off))
            alpha = jnp.exp2(m[:, cs] - m_new)
            m_out.append(m_new)
            acc_scr[:, cs] = acc_scr[:, cs] * alpha + _dot(vt, p.astype(BF16))
        return jnp.concatenate(m_out, axis=1)

    def blocks(first, count, m):
        for u in range(0, count, 2):
            kj = first + u
            scores(kj + 1, s1_scr, None)
            m = consume(kj, s0_scr, m)
            scores(kj + 2, s0_scr, None)
            m = consume(kj + 1, s1_scr, m)
        return m

    m0 = jnp.full((1, 2 * tq), -jnp.inf, F32)
    scores(0, s0_scr, None)
    n_full = qi * n_sub
    n_big = n_full // LOOP_BLOCKS
    m = lax.fori_loop(0, n_big, lambda j, m: blocks(LOOP_BLOCKS * j, LOOP_BLOCKS, m), m0)
    m = lax.cond(n_full - n_big * LOOP_BLOCKS > 0,
                 lambda m: blocks(n_big * LOOP_BLOCKS, LOOP_BLOCKS // 2, m), lambda m: m, m)

    dbufs = (s0_scr, s1_scr) + tuple(sd_scr)
    for sub in range(1, n_sub):
        scores(n_full + sub, dbufs[sub], sub)
    causal = lax.broadcasted_iota(jnp.int32, (tk, tk), 0) <= lax.broadcasted_iota(jnp.int32, (tk, tk), 1)
    offs = [slope_l2 * ((n_full + sub) * tk).astype(F32) for sub in range(n_sub)]
    for cs, pos in slabs:
        s_blk = [dbufs[sub][:, cs] for sub in range(pos)] + [jnp.where(causal, dbufs[pos][:, cs], -jnp.inf)]
        m_old = m[:, cs]
        m_new = m_old
        for sub in range(pos + 1):
            m_new = jnp.maximum(m_new, jnp.max(s_blk[sub], axis=0, keepdims=True) + offs[sub])
        pv = None
        for sub in range(pos + 1):
            p = jnp.exp2(s_blk[sub] - (m_new - offs[sub])).astype(BF16)
            k0 = pl.multiple_of((n_full + sub) * tk, tk)
            d = _dot(vt_scr[:, pl.ds(k0, tk)], p)
            pv = d if pv is None else pv + d
        acc_scr[:, cs] = acc_scr[:, cs] * jnp.exp2(m_old - m_new) + pv

    lv = lam_ref[...]
    lam = (jnp.exp(jnp.sum(lv[0:1] * lv[1:2], axis=-1, keepdims=True))
           - jnp.exp(jnp.sum(lv[2:3] * lv[3:4], axis=-1, keepdims=True)) + lam_init)
    inv_l = 1.0 / acc_scr[DIFF_DV:DIFF_DV + 1, :]
    o = (acc_scr[0:DIFF_DV, 0:tq] * inv_l[:, :tq] - acc_scr[0:DIFF_DV, tq:] * (lam * inv_l[:, tq:])).T
    on = o * lax.rsqrt(jnp.mean(o * o, axis=-1, keepdims=True) + NORM_EPS) * nw_ref[...] * (1.0 - lam_init)
    z = z_ref[...].astype(F32)
    o_ref[...] = (on * (z * _sigmoid(z))).astype(o_ref.dtype)


def _diff_attn(proj, lam_vecs, nw, *, batch, seq, lam_init, tq=1024, tk=256):
    m = batch * seq
    nq = seq // tq
    slopes = 2.0 ** (-8.0 * jnp.arange(1, DIFF_HEADS + 1, dtype=F32) / DIFF_HEADS)
    slopes = jnp.broadcast_to(slopes[:, None, None], (DIFF_HEADS, 1, LANES))
    return pl.pallas_call(
        functools.partial(_diff_kernel, tq=tq, tk=tk, lam_init=lam_init),
        grid=(batch, DIFF_HEADS, nq),
        in_specs=[
            pl.BlockSpec((4, DIFF_DH), lambda b, h, i: (0, 0)),
            pl.BlockSpec((1, DIFF_DV), lambda b, h, i: (0, 0)),
            pl.BlockSpec((1, 1, LANES), lambda b, h, i: (h, 0, 0)),
            pl.BlockSpec((tq, DIFF_DV), lambda b, h, i: (b * nq + i, COL_DIFF_Q + h)),
            pl.BlockSpec((seq, DIFF_DV), lambda b, h, i: (b, COL_DIFF_K + h)),
            pl.BlockSpec((seq, DIFF_DV), lambda b, h, i: (b, COL_DIFF_V + h)),
            pl.BlockSpec((tq, DIFF_DV), lambda b, h, i: (b * nq + i, COL_DIFF_Z + h)),
        ],
        out_specs=pl.BlockSpec((tq, DIFF_DV), lambda b, h, i: (b * nq + i, h)),
        out_shape=jax.ShapeDtypeStruct((m, DIFF_HEADS * DIFF_DV), BF16),
        scratch_shapes=[pltpu.VMEM((DIFF_DV + ONES_ROWS, seq), BF16),
                        pltpu.VMEM((DIFF_DV + ONES_ROWS, 2 * tq), F32),
                        ] + [pltpu.VMEM((tk, 2 * tq), F32)] * (tq // tk),
        name="diffattn",
        compiler_params=pltpu.CompilerParams(
            dimension_semantics=("arbitrary", "arbitrary", "arbitrary"), vmem_limit_bytes=VMEM_LIMIT),
    )(lam_vecs, nw, slopes, proj, proj, proj, proj)


def _memkv_kernel(mem_ref, nw_ref, w_ref, o_ref):
    x = mem_ref[...]
    ms = jnp.mean(x * x, axis=-1, keepdims=True)
    hb = ((x * lax.rsqrt(ms + NORM_EPS)) * nw_ref[...]).astype(BF16)
    o_ref[...] = _dot(hb, w_ref[...]).astype(o_ref.dtype)


def _memkv(mem2d, nw, w, *, tn=1024):
    rows = mem2d.shape[0]
    width = w.shape[1]
    return pl.pallas_call(
        _memkv_kernel,
        grid=(width // tn,),
        in_specs=[
            pl.BlockSpec((rows, D_MODEL), lambda j: (0, 0)),
            pl.BlockSpec((1, D_MODEL), lambda j: (0, 0)),
            pl.BlockSpec((D_MODEL, tn), lambda j: (0, j)),
        ],
        out_specs=pl.BlockSpec((rows, tn), lambda j: (0, j)),
        out_shape=jax.ShapeDtypeStruct((rows, width), BF16),
        name="memkv",
        compiler_params=pltpu.CompilerParams(
            dimension_semantics=("arbitrary",), vmem_limit_bytes=VMEM_LIMIT),
    )(mem2d, nw, w)


def _merge_kernel(x_ref, oa_ref, ob_ref, mq_ref, mz_ref, gg_ref, gd_ref, gm_ref, mk_ref, mv_ref,
                  wg_ref, wd_ref, wm_ref, wo_ref, pw_ref, o_ref):
    heads = []
    for h in range(MEM_HEADS):
        sl = slice(h * MEM_DH, (h + 1) * MEM_DH)
        s = _dot_nt(mq_ref[:, sl], mk_ref[:, sl]) * (MEM_DH ** -0.5)
        p = jnp.exp(s - jnp.max(s, axis=-1, keepdims=True))
        p = p / jnp.sum(p, axis=-1, keepdims=True)
        heads.append(_dot(p.astype(BF16), mv_ref[:, sl]))
    mz = mz_ref[...].astype(F32)
    oc = (jnp.concatenate(heads, axis=1) * (mz * _sigmoid(mz))).astype(BF16)

    y = _sigmoid(gg_ref[...].astype(F32)) * _dot(oa_ref[...], wg_ref[...])
    y = y + _sigmoid(gd_ref[...].astype(F32)) * _dot(ob_ref[...], wd_ref[...])
    y = y + _sigmoid(gm_ref[...].astype(F32)) * _dot(oc, wm_ref[...])
    out = _dot(y.astype(BF16), wo_ref[...])
    ms = jnp.mean(out * out, axis=-1, keepdims=True)
    o_ref[...] = x_ref[...] + (out * lax.rsqrt(ms + NORM_EPS)) * pw_ref[...]


def _merge(x, proj, oa, ob, memkv, wg, wd, wm, wo, pw, *, seq, mem_len, tm=512):
    m = x.shape[0]
    per_b = seq // tm
    mem_w = MEM_HEADS * MEM_DH
    tok = lambda width, col: pl.BlockSpec((tm, width), lambda i, col=col: (i, col))
    wspec = pl.BlockSpec((D_MODEL, D_MODEL), lambda i: (0, 0))
    return pl.pallas_call(
        _merge_kernel,
        grid=(m // tm,),
        in_specs=[
            tok(D_MODEL, 0), tok(D_MODEL, 0), tok(D_MODEL, 0),
            tok(mem_w, COL_MEM_Q), tok(mem_w, COL_MEM_Z),
            tok(D_MODEL, COL_GATES), tok(D_MODEL, COL_GATES + 1), tok(D_MODEL, COL_GATES + 2),
            pl.BlockSpec((mem_len, mem_w), lambda i: (i // per_b, 0)),
            pl.BlockSpec((mem_len, mem_w), lambda i: (i // per_b, 1)),
            wspec, wspec, wspec, wspec,
            pl.BlockSpec((1, D_MODEL), lambda i: (0, 0)),
        ],
        out_specs=pl.BlockSpec((tm, D_MODEL), lambda i: (i, 0)),
        out_shape=jax.ShapeDtypeStruct((m, D_MODEL), F32),
        name="merge",
        compiler_params=pltpu.CompilerParams(
            dimension_semantics=("arbitrary",), vmem_limit_bytes=VMEM_LIMIT),
    )(x, oa, ob, proj, proj, proj, proj, proj, memkv, memkv, wg, wd, wm, wo, pw)


def kernel(x, mem, pre_norm_w, post_norm_w, w_in, gdn_conv_w, gdn_a_log, gdn_dt_bias, gdn_norm_w,
           diff_lambda, diff_norm_w, mem_norm_w, w_mem_kv, w_br_gdn, w_br_diff, w_br_mem, w_out):
    batch, seq, d = x.shape
    mem_len = mem.shape[1]
    depth = w_in.shape[0]
    assert d == D_MODEL and seq % 1024 == 0

    w_in_t = jnp.swapaxes(w_in, 1, 2)
    hw = GDN_HEADS * GDN_D
    conv_w = gdn_conv_w.reshape(depth, GDN_CONV, 3, hw).transpose(0, 2, 1, 3)
    lane_pad = lambda t: jnp.pad(t, ((0, 0), (0, LANES - GDN_HEADS)))[:, None, :]
    a_log = lane_pad(gdn_a_log)
    dt_bias = lane_pad(gdn_dt_bias)
    w_mem_kv_b = w_mem_kv.astype(BF16)
    w_g, w_d, w_m, w_o = (t.astype(BF16) for t in (w_br_gdn, w_br_diff, w_br_mem, w_out))

    xf = x.reshape(batch * seq, d)
    mem2d = mem.reshape(batch * mem_len, d)
    for l in range(depth):
        lam_init = 0.8 - 0.6 * math.exp(-0.3 * l)
        proj, ab = _inproj(xf, pre_norm_w[l][None], w_in_t, l)
        oa = _gdn(proj, ab, conv_w[l], a_log[l], dt_bias[l], gdn_norm_w[l][None], batch=batch, seq=seq)
        ob = _diff_attn(proj, diff_lambda[l], diff_norm_w[l][None], batch=batch, seq=seq, lam_init=lam_init)
        memkv = _memkv(mem2d, mem_norm_w[l][None], w_mem_kv_b[l])
        xf = _merge(xf, proj, oa, ob, memkv, w_g[l], w_d[l], w_m[l], w_o[l], post_norm_w[l][None],
                    seq=seq, mem_len=mem_len)
    return xf.reshape(batch, seq, d)
```

```python
import functools
import math

import jax
import jax.numpy as jnp
from jax import lax
from jax.experimental import pallas as pl
from jax.experimental.pallas import tpu as pltpu

F32 = jnp.float32
BF16 = jnp.bfloat16

D_MODEL = 1024
NORM_EPS = 1e-6
GDN_HEADS = 8
GDN_D = 128
GDN_CONV = 4
GDN_CHUNK = 64
DIFF_HEADS = 8
DIFF_DH = 64
DIFF_DV = 2 * DIFF_DH
MEM_HEADS = 4
MEM_DH = 256
N_BRANCH = 3

LANES = 128
LOG2E = 1.4426950408889634
ONES_ROWS = 16
LOOP_BLOCKS = 8
QKV_W = 3 * GDN_HEADS * GDN_D
COL_GDN_Q, COL_GDN_K, COL_GDN_V, COL_GDN_Z = 0, 1, 2, 3
COL_DIFF_Q, COL_DIFF_K, COL_DIFF_V, COL_DIFF_Z = 32, 40, 48, 56
COL_MEM_Q, COL_MEM_Z = 8, 9
COL_GATES = 10
PROJ_W = 13 * D_MODEL
AB_W = 2 * LANES
AB_SKIP = 2 * GDN_HEADS

VMEM_LIMIT = 56 * 1024 * 1024


def _sigmoid(x):
    return 1.0 / (1.0 + jnp.exp(-x))


def _softplus(x):
    return jnp.maximum(x, 0.0) + jnp.log(1.0 + jnp.exp(-jnp.abs(x)))


def _dot(a, b):
    return jnp.dot(a, b, preferred_element_type=F32)


def _dot_nt(a, b):
    return lax.dot_general(a, b, (((1,), (1,)), ((), ())), preferred_element_type=F32)


def _dot_tn(a, b):
    return lax.dot_general(a, b, (((0,), (0,)), ((), ())), preferred_element_type=F32)


def _inproj_kernel(x_ref, nw_ref, wt_ref, wabt_ref, o_ref, oab_ref, h_scr):
    j = pl.program_id(1)

    @pl.when(j == 0)
    def _():
        x = x_ref[...]
        ms = jnp.mean(x * x, axis=-1, keepdims=True)
        hb = ((x * lax.rsqrt(ms + NORM_EPS)) * nw_ref[...]).astype(BF16)
        h_scr[...] = hb
        pad = jnp.zeros((LANES - GDN_HEADS, D_MODEL), F32)
        wab_t = jnp.concatenate([wabt_ref[0, 0:GDN_HEADS], pad, wabt_ref[0, GDN_HEADS:AB_SKIP], pad], axis=0)
        oab_ref[...] = _dot_nt(hb, wab_t.astype(BF16))

    o_ref[...] = _dot_nt(h_scr[...], wt_ref[0].astype(BF16)).astype(o_ref.dtype)


def _inproj(x, nw, w_in_t, layer, *, tm=2048, tn=1024):
    m = x.shape[0]
    tm = min(tm, m)
    n_qkv = QKV_W // tn
    assert n_qkv * tn == QKV_W and w_in_t.shape[1] == PROJ_W + AB_SKIP
    return pl.pallas_call(
        _inproj_kernel,
        grid=(m // tm, PROJ_W // tn),
        in_specs=[
            pl.BlockSpec((tm, D_MODEL), lambda i, j: (i, 0)),
            pl.BlockSpec((1, D_MODEL), lambda i, j: (0, 0)),
            pl.BlockSpec((pl.Element(1), pl.Element(tn), pl.Element(D_MODEL)),
                         lambda i, j: (layer, pl.multiple_of(j * tn + jnp.where(j >= n_qkv, AB_SKIP, 0), 8), 0)),
            pl.BlockSpec((pl.Element(1), pl.Element(AB_SKIP), pl.Element(D_MODEL)),
                         lambda i, j: (layer, QKV_W, 0)),
        ],
        out_specs=[
            pl.BlockSpec((tm, tn), lambda i, j: (i, j)),
            pl.BlockSpec((tm, AB_W), lambda i, j: (i, 0)),
        ],
        out_shape=[
            jax.ShapeDtypeStruct((m, PROJ_W), BF16),
            jax.ShapeDtypeStruct((m, AB_W), F32),
        ],
        scratch_shapes=[pltpu.VMEM((tm, D_MODEL), BF16)],
        name="inproj",
        compiler_params=pltpu.CompilerParams(
            dimension_semantics=("arbitrary", "arbitrary"), vmem_limit_bytes=VMEM_LIMIT),
    )(x, nw, w_in_t, w_in_t)


def _gdn_kernel(q_ref, k_ref, v_ref, z_ref, a_ref, b_ref, cw_ref, alog_ref, dtb_ref, nw_ref, o_ref,
                state_scr, xpad, qs, ks, vs, gcol, bcol, grow, brow, us, wqe, attn_s, kd_s, *, t_blk):
    c_len = GDN_CHUNK
    n_chunk = t_blk // c_len
    hw = GDN_HEADS * GDN_D

    @pl.when(pl.program_id(1) == 0)
    def _():
        state_scr[...] = jnp.zeros_like(state_scr)
        xpad[:, 0:8, :] = jnp.zeros((3, 8, hw), F32)

    def conv_silu(ref, idx):
        xpad[idx, 8:8 + t_blk, :] = ref[...].astype(F32)
        w = cw_ref[idx]
        y = w[GDN_CONV - 1:GDN_CONV] * xpad[idx, 8:8 + t_blk, :]
        for j in range(GDN_CONV - 1):
            y = y + w[j:j + 1] * xpad[idx, 5 + j:5 + j + t_blk, :]
        xpad[idx, 0:8, :] = xpad[idx, t_blk:t_blk + 8, :]
        return y * _sigmoid(y)

    yq = conv_silu(q_ref, 0)
    yk = conv_silu(k_ref, 1)
    vs[...] = conv_silu(v_ref, 2)
    for h in range(GDN_HEADS):
        sl = slice(h * GDN_D, (h + 1) * GDN_D)
        qh = yq[:, sl]
        kh = yk[:, sl]
        qs[:, sl] = qh * (lax.rsqrt(jnp.sum(qh * qh, axis=-1, keepdims=True) + NORM_EPS) * (GDN_D ** -0.5))
        ks[:, sl] = kh * lax.rsqrt(jnp.sum(kh * kh, axis=-1, keepdims=True) + NORM_EPS)

    g = -jnp.exp(alog_ref[...]) * _softplus(a_ref[...] + dtb_ref[...])
    beta = _sigmoid(b_ref[...])
    rin = lax.broadcasted_iota(jnp.int32, (t_blk, LANES), 0) & (c_len - 1)
    s = 1
    while s < c_len:
        g = g + jnp.where(rin >= s, pltpu.roll(g, s, axis=0), 0.0)
        s *= 2
    gcol[...] = g
    bcol[...] = beta
    g_t = g.T
    b_t = beta.T
    for c in range(n_chunk):
        cols = slice(c * c_len, (c + 1) * c_len)
        grow[c] = g_t[0:8, cols]
        brow[c] = jnp.concatenate([b_t[0:8, cols], b_t[0:8, cols]], axis=1)

    ii = lax.broadcasted_iota(jnp.int32, (c_len, c_len), 0)
    jj = lax.broadcasted_iota(jnp.int32, (c_len, c_len), 1)
    incl = ii >= jj
    strict = ii > jj
    eye = jnp.where(ii == jj, 1.0, 0.0).astype(F32)
    lane_lo = lax.broadcasted_iota(jnp.int32, (c_len, 2 * c_len), 1) < c_len
    zeros_w = jnp.zeros((c_len, 2 * c_len), BF16)
    zeros_rhs = jnp.zeros((c_len, 2 * GDN_D), BF16)
    head_sl = [slice(h * GDN_D, (h + 1) * GDN_D) for h in range(GDN_HEADS)]

    chains = []
    for c in range(n_chunk):
        rows = slice(c * c_len, (c + 1) * c_len)
        g_c = gcol[rows, :]
        b_c = bcol[rows, :]
        g_r = grow[c]
        b_r = brow[c]
        for h in range(GDN_HEADS):
            chains.append(dict(
                idx=c * GDN_HEADS + h, rows=rows, sl=head_sl[h],
                q=qs[rows, head_sl[h]], k=ks[rows, head_sl[h]], v=vs[rows, head_sl[h]],
                gi=g_c[:, h:h + 1], gj=g_r[h:h + 1, :], bi=b_c[:, h:h + 1], bj2=b_r[h:h + 1, :],
                g_last=g_c[c_len - 1:c_len, h:h + 1]))
    for x in chains:
        kb = x["k"].astype(BF16)
        x["qk"] = _dot_nt(jnp.concatenate([x["q"].astype(BF16), kb], axis=0), kb)
    for x in chains:
        gam = jnp.exp(jnp.where(incl, x["gi"] - x["gj"], -jnp.inf))
        x["attn"] = (x["qk"][:c_len] * gam).astype(BF16)
        a_mat = jnp.where(strict, x["bi"] * x["qk"][c_len:] * gam, 0.0)
        x["w"] = jnp.concatenate([-a_mat, eye], axis=1)
    for _ in range(6):
        for x in chains:
            wb = x["w"].astype(BF16)
            x["r"] = _dot(wb, jnp.concatenate([wb, zeros_w], axis=0))
        for x in chains:
            x["w"] = jnp.where(lane_lo, x["r"], x["w"] + x["r"])
    for x in chains:
        eg = jnp.exp(x["gi"])
        rhs = jnp.concatenate([x["v"], x["k"] * eg], axis=1).astype(BF16)
        t_mat = (x["w"] * x["bj2"]).astype(BF16)
        x["uw"] = _dot(t_mat, jnp.concatenate([zeros_rhs, rhs], axis=0))
        x["qe"] = (x["q"] * eg).astype(BF16)
        x["kd"] = (x["k"] * jnp.exp(x["g_last"] - x["gi"])).astype(BF16)
    for x in chains:
        us[x["rows"], x["sl"]] = x["uw"][:, :GDN_D]
        wqe[x["idx"], 0:c_len, :] = x["uw"][:, GDN_D:].astype(BF16)
        wqe[x["idx"], c_len:2 * c_len, :] = x["qe"]
        attn_s[x["idx"]] = x["attn"]
        kd_s[x["idx"]] = x["kd"]

    st = [state_scr[h] for h in range(GDN_HEADS)]
    for c in range(n_chunk):
        rows = slice(c * c_len, (c + 1) * c_len)
        g_c = gcol[rows, :]
        idx = [c * GDN_HEADS + h for h in range(GDN_HEADS)]
        wq = [_dot(wqe[idx[h]], st[h].astype(BF16)) for h in range(GDN_HEADS)]
        v_new = [(us[rows, head_sl[h]] - wq[h][:c_len]).astype(BF16) for h in range(GDN_HEADS)]
        o = [wq[h][c_len:] + _dot(attn_s[idx[h]], v_new[h]) for h in range(GDN_HEADS)]
        st = [st[h] * jnp.exp(g_c[c_len - 1:c_len, h:h + 1]) + _dot_tn(kd_s[idx[h]], v_new[h])
              for h in range(GDN_HEADS)]
        for h in range(GDN_HEADS):
            us[rows, head_sl[h]] = o[h]
    for h in range(GDN_HEADS):
        state_scr[h] = st[h]

    nw = nw_ref[...]
    for h in range(GDN_HEADS):
        o = us[:, head_sl[h]]
        on = o * lax.rsqrt(jnp.mean(o * o, axis=-1, keepdims=True) + NORM_EPS) * nw
        z = z_ref[:, head_sl[h]].astype(F32)
        o_ref[:, head_sl[h]] = (on * (z * _sigmoid(z))).astype(o_ref.dtype)


def _gdn(proj, ab, cw, alog, dtb, nw, *, batch, seq, t_blk=256):
    m = batch * seq
    hw = GDN_HEADS * GDN_D
    nt = seq // t_blk
    n_chunk = t_blk // GDN_CHUNK
    tok = lambda col: pl.BlockSpec((t_blk, hw), lambda b, t, col=col: (b * nt + t, col))
    const2 = lambda shape: pl.BlockSpec(shape, lambda b, t: (0,) * len(shape))
    return pl.pallas_call(
        functools.partial(_gdn_kernel, t_blk=t_blk),
        grid=(batch, nt),
        in_specs=[
            tok(COL_GDN_Q), tok(COL_GDN_K), tok(COL_GDN_V), tok(COL_GDN_Z),
            pl.BlockSpec((t_blk, LANES), lambda b, t: (b * nt + t, 0)),
            pl.BlockSpec((t_blk, LANES), lambda b, t: (b * nt + t, 1)),
            const2((3, GDN_CONV, hw)), const2((1, LANES)), const2((1, LANES)), const2((1, GDN_D)),
        ],
        out_specs=pl.BlockSpec((t_blk, hw), lambda b, t: (b * nt + t, 0)),
        out_shape=jax.ShapeDtypeStruct((m, hw), BF16),
        scratch_shapes=[
            pltpu.VMEM((GDN_HEADS, GDN_D, GDN_D), F32),
            pltpu.VMEM((3, 8 + t_blk, hw), F32),
            pltpu.VMEM((t_blk, hw), F32), pltpu.VMEM((t_blk, hw), F32), pltpu.VMEM((t_blk, hw), F32),
            pltpu.VMEM((t_blk, LANES), F32), pltpu.VMEM((t_blk, LANES), F32),
            pltpu.VMEM((n_chunk, 8, GDN_CHUNK), F32), pltpu.VMEM((n_chunk, 8, 2 * GDN_CHUNK), F32),
            pltpu.VMEM((t_blk, hw), F32),
            pltpu.VMEM((n_chunk * GDN_HEADS, 2 * GDN_CHUNK, GDN_D), BF16),
            pltpu.VMEM((n_chunk * GDN_HEADS, GDN_CHUNK, GDN_CHUNK), BF16),
            pltpu.VMEM((n_chunk * GDN_HEADS, GDN_CHUNK, GDN_D), BF16),
        ],
        name="gdn",
        compiler_params=pltpu.CompilerParams(
            dimension_semantics=("arbitrary", "arbitrary"), vmem_limit_bytes=VMEM_LIMIT),
    )(proj, proj, proj, proj, ab, ab, cw, alog, dtb, nw)


def _diff_kernel(lam_ref, nw_ref, slope_ref, q_ref, k_ref, v_ref, z_ref, o_ref, vt_scr, acc_scr,
                 s0_scr, s1_scr, *sd_scr, tq, tk, lam_init):
    qi = pl.program_id(2)
    n_sub = tq // tk
    assert n_sub * tk == tq and n_sub % (LOOP_BLOCKS // 2) == 0 and LOOP_BLOCKS % 4 == 0

    @pl.when(qi == 0)
    def _():
        vt_scr[0:DIFF_DV, :] = v_ref[...].astype(F32).T.astype(BF16)
        vt_scr[DIFF_DV:, :] = jnp.ones((ONES_ROWS, vt_scr.shape[1]), BF16)

    q_t = (q_ref[...].astype(F32) * (DIFF_DH ** -0.5 * LOG2E)).T
    row = lax.broadcasted_iota(jnp.int32, (2 * DIFF_DH, tq), 0)
    qq = jnp.concatenate([jnp.where(row < DIFF_DH, q_t, 0.0), jnp.where(row >= DIFF_DH, q_t, 0.0)],
                         axis=1).astype(BF16)
    slope = slope_ref[0]
    key_pos = lax.broadcasted_iota(jnp.int32, (tk, LANES), 0).astype(F32)
    lane = lax.broadcasted_iota(jnp.int32, (tk, LANES), 1)
    b_full = slope * key_pos
    b_hi = b_full.astype(BF16).astype(F32)
    b_lo = b_full - b_hi
    k_bias = jnp.where(lane < 2, b_hi, jnp.where(lane < 4, b_lo, 0.0)).astype(BF16)
    c_full = jnp.full((LANES, 2 * tq), LOG2E, F32)
    c_hi = c_full.astype(BF16).astype(F32)
    c_lo = c_full - c_hi
    row2 = lax.broadcasted_iota(jnp.int32, (LANES, 2 * tq), 0)
    q_bias = jnp.where(row2 >= 4, 0.0, jnp.where((row2 & 1) == 0, c_hi, c_lo)).astype(BF16)
    qq = jnp.concatenate([qq, q_bias], axis=0)
    slope_l2 = slope[:, 0:1] * LOG2E
    acc_scr[...] = jnp.zeros_like(acc_scr)

    slabs = [(slice(c * tk, (c + 1) * tk), c % n_sub) for c in range(2 * n_sub)]

    def scores(kj, s_ref, sub):
        k0 = pl.multiple_of(kj * tk, tk)
        ka = jnp.concatenate([k_ref[pl.ds(k0, tk), :], k_bias], axis=1)
        if sub is None:
            s_ref[...] = _dot(ka, qq)
        else:
            for cs, pos in slabs:
                if pos >= sub:
                    s_ref[:, cs] = _dot(ka, qq[:, cs])

    def consume(kj, s_ref, m):
        k0 = pl.multiple_of(kj * tk, tk)
        off = slope_l2 * (kj * tk).astype(F32)
        vt = vt_scr[:, pl.ds(k0, tk)]
        m_out = []
        for cs, _ in slabs:
            s = s_ref[:, cs]
            m_new = jnp.maximum(m[:, cs], jnp.max(s, axis=0, keepdims=True) + off)
            p = jnp.exp2(s - (m_new - off))
            alpha = jnp.exp2(m[:, cs] - m_new)
            m_out.append(m_new)
            acc_scr[:, cs] = acc_scr[:, cs] * alpha + _dot(vt, p.astype(BF16))
        return jnp.concatenate(m_out, axis=1)

    def blocks(first, count, m):
        for u in range(0, count, 2):
            kj = first + u
            scores(kj + 1, s1_scr, None)
            m = consume(kj, s0_scr, m)
            scores(kj + 2, s0_scr, None)
            m = consume(kj + 1, s1_scr, m)
        return m

    m0 = jnp.full((1, 2 * tq), -jnp.inf, F32)
    scores(0, s0_scr, None)
    n_full = qi * n_sub
    n_big = n_full // LOOP_BLOCKS
    m = lax.fori_loop(0, n_big, lambda j, m: blocks(LOOP_BLOCKS * j, LOOP_BLOCKS, m), m0)
    m = lax.cond(n_full - n_big * LOOP_BLOCKS > 0,
                 lambda m: blocks(n_big * LOOP_BLOCKS, LOOP_BLOCKS // 2, m), lambda m: m, m)

    dbufs = (s0_scr, s1_scr) + tuple(sd_scr)
    for sub in range(1, n_sub):
        scores(n_full + sub, dbufs[sub], sub)
    causal = lax.broadcasted_iota(jnp.int32, (tk, tk), 0) <= lax.broadcasted_iota(jnp.int32, (tk, tk), 1)
    offs = [slope_l2 * ((n_full + sub) * tk).astype(F32) for sub in range(n_sub)]
    for cs, pos in slabs:
        s_blk = [dbufs[sub][:, cs] for sub in range(pos)] + [jnp.where(causal, dbufs[pos][:, cs], -jnp.inf)]
        m_old = m[:, cs]
        m_new = m_old
        for sub in range(pos + 1):
            m_new = jnp.maximum(m_new, jnp.max(s_blk[sub], axis=0, keepdims=True) + offs[sub])
        pv = None
        for sub in range(pos + 1):
            p = jnp.exp2(s_blk[sub] - (m_new - offs[sub])).astype(BF16)
            k0 = pl.multiple_of((n_full + sub) * tk, tk)
            d = _dot(vt_scr[:, pl.ds(k0, tk)], p)
            pv = d if pv is None else pv + d
        acc_scr[:, cs] = acc_scr[:, cs] * jnp.exp2(m_old - m_new) + pv

    lv = lam_ref[...]
    lam = (jnp.exp(jnp.sum(lv[0:1] * lv[1:2], axis=-1, keepdims=True))
           - jnp.exp(jnp.sum(lv[2:3] * lv[3:4], axis=-1, keepdims=True)) + lam_init)
    inv_l = 1.0 / acc_scr[DIFF_DV:DIFF_DV + 1, :]
    o = (acc_scr[0:DIFF_DV, 0:tq] * inv_l[:, :tq] - acc_scr[0:DIFF_DV, tq:] * (lam * inv_l[:, tq:])).T
    on = o * lax.rsqrt(jnp.mean(o * o, axis=-1, keepdims=True) + NORM_EPS) * nw_ref[...] * (1.0 - lam_init)
    z = z_ref[...].astype(F32)
    o_ref[...] = (on * (z * _sigmoid(z))).astype(o_ref.dtype)


def _diff_attn(proj, lam_vecs, nw, *, batch, seq, lam_init, tq=1024, tk=256):
    m = batch * seq
    nq = seq // tq
    slopes = 2.0 ** (-8.0 * jnp.arange(1, DIFF_HEADS + 1, dtype=F32) / DIFF_HEADS)
    slopes = jnp.broadcast_to(slopes[:, None, None], (DIFF_HEADS, 1, LANES))
    return pl.pallas_call(
        functools.partial(_diff_kernel, tq=tq, tk=tk, lam_init=lam_init),
        grid=(batch, DIFF_HEADS, nq),
        in_specs=[
            pl.BlockSpec((4, DIFF_DH), lambda b, h, i: (0, 0)),
            pl.BlockSpec((1, DIFF_DV), lambda b, h, i: (0, 0)),
            pl.BlockSpec((1, 1, LANES), lambda b, h, i: (h, 0, 0)),
            pl.BlockSpec((tq, DIFF_DV), lambda b, h, i: (b * nq + i, COL_DIFF_Q + h)),
            pl.BlockSpec((seq, DIFF_DV), lambda b, h, i: (b, COL_DIFF_K + h)),
            pl.BlockSpec((seq, DIFF_DV), lambda b, h, i: (b, COL_DIFF_V + h)),
            pl.BlockSpec((tq, DIFF_DV), lambda b, h, i: (b * nq + i, COL_DIFF_Z + h)),
        ],
        out_specs=pl.BlockSpec((tq, DIFF_DV), lambda b, h, i: (b * nq + i, h)),
        out_shape=jax.ShapeDtypeStruct((m, DIFF_HEADS * DIFF_DV), BF16),
        scratch_shapes=[pltpu.VMEM((DIFF_DV + ONES_ROWS, seq), BF16),
                        pltpu.VMEM((DIFF_DV + ONES_ROWS, 2 * tq), F32),
                        ] + [pltpu.VMEM((tk, 2 * tq), F32)] * (tq // tk),
        name="diffattn",
        compiler_params=pltpu.CompilerParams(
            dimension_semantics=("arbitrary", "arbitrary", "arbitrary"), vmem_limit_bytes=VMEM_LIMIT),
    )(lam_vecs, nw, slopes, proj, proj, proj, proj)


def _memkv_kernel(mem_ref, nw_ref, w_ref, o_ref):
    x = mem_ref[...]
    ms = jnp.mean(x * x, axis=-1, keepdims=True)
    hb = ((x * lax.rsqrt(ms + NORM_EPS)) * nw_ref[...]).astype(BF16)
    o_ref[...] = _dot(hb, w_ref[...]).astype(o_ref.dtype)


def _memkv(mem2d, nw, w, *, tn=1024):
    rows = mem2d.shape[0]
    width = w.shape[1]
    return pl.pallas_call(
        _memkv_kernel,
        grid=(width // tn,),
        in_specs=[
            pl.BlockSpec((rows, D_MODEL), lambda j: (0, 0)),
            pl.BlockSpec((1, D_MODEL), lambda j: (0, 0)),
            pl.BlockSpec((D_MODEL, tn), lambda j: (0, j)),
        ],
        out_specs=pl.BlockSpec((rows, tn), lambda j: (0, j)),
        out_shape=jax.ShapeDtypeStruct((rows, width), BF16),
        name="memkv",
        compiler_params=pltpu.CompilerParams(
            dimension_semantics=("arbitrary",), vmem_limit_bytes=VMEM_LIMIT),
    )(mem2d, nw, w)


def _merge_kernel(x_ref, oa_ref, ob_ref, mq_ref, mz_ref, gg_ref, gd_ref, gm_ref, mk_ref, mv_ref,
                  wg32_ref, wd32_ref, wm32_ref, wo32_ref, pw_ref, o_ref, wb_scr):
    @pl.when(pl.program_id(0) == 0)
    def _():
        for n, w32 in enumerate((wg32_ref, wd32_ref, wm32_ref, wo32_ref)):
            wb_scr[n] = w32[...].astype(BF16)

    wg_ref, wd_ref, wm_ref, wo_ref = (wb_scr.at[n] for n in range(4))
    heads = []
    for h in range(MEM_HEADS):
        sl = slice(h * MEM_DH, (h + 1) * MEM_DH)
        s = _dot_nt(mq_ref[:, sl], mk_ref[:, sl]) * (MEM_DH ** -0.5)
        p = jnp.exp(s - jnp.max(s, axis=-1, keepdims=True))
        p = p / jnp.sum(p, axis=-1, keepdims=True)
        heads.append(_dot(p.astype(BF16), mv_ref[:, sl]))
    mz = mz_ref[...].astype(F32)
    oc = (jnp.concatenate(heads, axis=1) * (mz * _sigmoid(mz))).astype(BF16)

    y = _sigmoid(gg_ref[...].astype(F32)) * _dot(oa_ref[...], wg_ref[...])
    y = y + _sigmoid(gd_ref[...].astype(F32)) * _dot(ob_ref[...], wd_ref[...])
    y = y + _sigmoid(gm_ref[...].astype(F32)) * _dot(oc, wm_ref[...])
    out = _dot(y.astype(BF16), wo_ref[...])
    ms = jnp.mean(out * out, axis=-1, keepdims=True)
    o_ref[...] = x_ref[...] + (out * lax.rsqrt(ms + NORM_EPS)) * pw_ref[...]


def _merge(x, proj, oa, ob, memkv, wg, wd, wm, wo, pw, *, seq, mem_len, tm=512):
    m = x.shape[0]
    per_b = seq // tm
    mem_w = MEM_HEADS * MEM_DH
    tok = lambda width, col: pl.BlockSpec((tm, width), lambda i, col=col: (i, col))
    wspec = pl.BlockSpec((D_MODEL, D_MODEL), lambda i: (0, 0), pipeline_mode=pl.Buffered(1))
    return pl.pallas_call(
        _merge_kernel,
        grid=(m // tm,),
        in_specs=[
            tok(D_MODEL, 0), tok(D_MODEL, 0), tok(D_MODEL, 0),
            tok(mem_w, COL_MEM_Q), tok(mem_w, COL_MEM_Z),
            tok(D_MODEL, COL_GATES), tok(D_MODEL, COL_GATES + 1), tok(D_MODEL, COL_GATES + 2),
            pl.BlockSpec((mem_len, mem_w), lambda i: (i // per_b, 0)),
            pl.BlockSpec((mem_len, mem_w), lambda i: (i // per_b, 1)),
            wspec, wspec, wspec, wspec,
            pl.BlockSpec((1, D_MODEL), lambda i: (0, 0)),
        ],
        out_specs=pl.BlockSpec((tm, D_MODEL), lambda i: (i, 0)),
        out_shape=jax.ShapeDtypeStruct((m, D_MODEL), F32),
        scratch_shapes=[pltpu.VMEM((4, D_MODEL, D_MODEL), BF16)],
        name="merge",
        compiler_params=pltpu.CompilerParams(
            dimension_semantics=("arbitrary",), vmem_limit_bytes=VMEM_LIMIT),
    )(x, oa, ob, proj, proj, proj, proj, proj, memkv, memkv, wg, wd, wm, wo, pw)


def kernel(x, mem, pre_norm_w, post_norm_w, w_in, gdn_conv_w, gdn_a_log, gdn_dt_bias, gdn_norm_w,
           diff_lambda, diff_norm_w, mem_norm_w, w_mem_kv, w_br_gdn, w_br_diff, w_br_mem, w_out):
    batch, seq, d = x.shape
    mem_len = mem.shape[1]
    depth = w_in.shape[0]
    assert d == D_MODEL and seq % 1024 == 0

    w_in_t = jnp.swapaxes(w_in, 1, 2)
    hw = GDN_HEADS * GDN_D
    conv_w = gdn_conv_w.reshape(depth, GDN_CONV, 3, hw).transpose(0, 2, 1, 3)
    lane_pad = lambda t: jnp.pad(t, ((0, 0), (0, LANES - GDN_HEADS)))[:, None, :]
    a_log = lane_pad(gdn_a_log)
    dt_bias = lane_pad(gdn_dt_bias)
    w_mem_kv_b = w_mem_kv.astype(BF16)
    w_g, w_d, w_m, w_o = w_br_gdn, w_br_diff, w_br_mem, w_out

    xf = x.reshape(batch * seq, d)
    mem2d = mem.reshape(batch * mem_len, d)
    for l in range(depth):
        lam_init = 0.8 - 0.6 * math.exp(-0.3 * l)
        proj, ab = _inproj(xf, pre_norm_w[l][None], w_in_t, l)
        oa = _gdn(proj, ab, conv_w[l], a_log[l], dt_bias[l], gdn_norm_w[l][None], batch=batch, seq=seq)
        ob = _diff_attn(proj, diff_lambda[l], diff_norm_w[l][None], batch=batch, seq=seq, lam_init=lam_init)
        memkv = _memkv(mem2d, mem_norm_w[l][None], w_mem_kv_b[l])
        xf = _merge(xf, proj, oa, ob, memkv, w_g[l], w_d[l], w_m[l], w_o[l], post_norm_w[l][None],
                    seq=seq, mem_len=mem_len)
    return xf.reshape(batch, seq, d)
```
